```python
import math
import jax, jax.numpy as jnp
from jax import lax
import numpy as np

D_MODEL = 1024
BATCH = 8
SEQ = 4096
DEPTH = 1
DEC_BATCH = 128
DEC_SEQ = 1
PAST_LEN = 8192
PAGE_SIZE = 128

N_GROUPS = 3
GROUP_WINDOWS = (128, 512, 2048)
GROUP_DILATIONS = (1, 4, 16)
HEADS_PER_GROUP = 8
HEAD_DIM = 64
ATTN_WIDTH = HEADS_PER_GROUP * HEAD_DIM
QKV_WIDTH = N_GROUPS * ATTN_WIDTH
CONV_CH = D_MODEL
CONV_WIDTH = 31
D_FF = 4 * D_MODEL
BLOCK = 128
IN_WIDTH = 3 * QKV_WIDTH + 2 * CONV_CH + 2 * D_MODEL
EPS = 1e-6
ATTN_SCALE = 1.0 / math.sqrt(HEAD_DIM)

kernel_name = "dilated_swa_conformer_conv_gated_hybrid_step"


def rmsnorm(x, g):
    xf = x.astype(jnp.float32)
    y = xf * lax.rsqrt(jnp.mean(xf * xf, axis=-1, keepdims=True) + EPS)
    return (y * g.astype(jnp.float32)).astype(x.dtype)


def layernorm(x, g, b):
    xf = x.astype(jnp.float32)
    mu = jnp.mean(xf, axis=-1, keepdims=True)
    var = jnp.mean(jnp.square(xf - mu), axis=-1, keepdims=True)
    y = (xf - mu) * lax.rsqrt(var + EPS)
    return (y * g.astype(jnp.float32) + b.astype(jnp.float32)).astype(x.dtype)


def mixer_inputs(x, norm1_g, w_in, q_norm_g, k_norm_g):
    B, T, _ = x.shape
    xn = rmsnorm(x, norm1_g)
    proj = xn @ w_in
    q, k, v, glu, gates = jnp.split(
        proj, [QKV_WIDTH, 2 * QKV_WIDTH, 3 * QKV_WIDTH, 3 * QKV_WIDTH + 2 * CONV_CH], axis=-1)
    hshape = (B, T, N_GROUPS, HEADS_PER_GROUP, HEAD_DIM)
    q = rmsnorm(q.reshape(hshape), q_norm_g[:, None, :])
    k = rmsnorm(k.reshape(hshape), k_norm_g[:, None, :])
    v = v.reshape(hshape)
    glu_a, glu_b = jnp.split(glu, 2, axis=-1)
    u = glu_a * jax.nn.sigmoid(glu_b)
    return q, k, v, u, gates


def dilated_attention_prompt(q, k, v, dilation, window):
    B, S, H, E = q.shape
    L = S // dilation
    n_keys = window // dilation
    nb = -(-L // BLOCK)
    Lp = nb * BLOCK
    N = B * dilation

    def to_blocks(t):
        t = t.reshape(B, L, dilation, H, E).transpose(0, 2, 1, 3, 4).reshape(N, L, H, E)
        t = jnp.pad(t, ((0, 0), (0, Lp - L), (0, 0), (0, 0)))
        return t.reshape(N, nb, BLOCK, H, E)

    def band(t):
        prev = jnp.pad(t[:, :-1], ((0, 0), (1, 0), (0, 0), (0, 0), (0, 0)))
        return jnp.concatenate([prev, t], axis=2)

    qb, kb, vb = to_blocks(q), to_blocks(k), to_blocks(v)
    kk, vv = band(kb), band(vb)
    s = jnp.einsum('nbqhe,nbkhe->nbhqk', qb.astype(jnp.float32), kk.astype(jnp.float32)) * ATTN_SCALE
    a = jnp.arange(BLOCK)[:, None]
    c = jnp.arange(2 * BLOCK)[None, :]
    dist = a + BLOCK - c
    blk = jnp.arange(nb)[:, None, None]
    valid = (dist >= 0) & (dist <= n_keys) & (blk * BLOCK - BLOCK + c >= 0)
    s = jnp.where(valid[None, :, None], s, -jnp.inf)
    lse = jax.nn.logsumexp(s, axis=-1)
    p = jnp.exp(s - lse[..., None])
    o = jnp.einsum('nbhqk,nbkhe->nbqhe', p, vv.astype(jnp.float32))
    o = o.reshape(N, Lp, H, E)[:, :L].reshape(B, dilation, L, H, E)
    o = o.transpose(0, 2, 1, 3, 4).reshape(B, S, H, E)
    lse = lse.transpose(0, 1, 3, 2).reshape(N, Lp, H)[:, :L].reshape(B, dilation, L, H)
    lse = lse.transpose(0, 2, 1, 3).reshape(B, S, H)
    return o, lse


def dilated_attention_sample(q, k_new, v_new, k_buf, v_buf, dilation, window):
    T = q.shape[1]
    Lb = k_buf.shape[1]
    kc = jnp.concatenate([k_buf, k_new], axis=1)
    vc = jnp.concatenate([v_buf, v_new], axis=1)
    m = jnp.arange(window // dilation + 1)
    idx = Lb + jnp.arange(T)[:, None] - m[None, :] * dilation
    valid = idx >= 0
    idx = jnp.clip(idx, 0)
    kg = kc[:, idx]
    vg = vc[:, idx]
    s = jnp.einsum('bthe,btmhe->bhtm', q.astype(jnp.float32), kg.astype(jnp.float32)) * ATTN_SCALE
    s = jnp.where(valid[None, None], s, -jnp.inf)
    lse = jax.nn.logsumexp(s, axis=-1)
    p = jnp.exp(s - lse[..., None])
    o = jnp.einsum('bhtm,btmhe->bthe', p, vg.astype(jnp.float32))
    keep = min(window, Lb + T)
    return o, lse.transpose(0, 2, 1), kc[:, -keep:], vc[:, -keep:]


def causal_dwconv(u_ext, conv_w, conv_b):
    y = lax.conv_general_dilated(
        u_ext, conv_w[:, None, :], window_strides=(1,), padding='VALID',
        dimension_numbers=('NWC', 'WIO', 'NWC'), feature_group_count=CONV_CH)
    return y + conv_b


def mixer_outputs(x, attn_outs, attn_lses, u_ext, gates, conv_w, conv_b, conv_ln_g, conv_ln_b,
                  w_attn_out, w_conv_out, w_o, norm2_g, w_ff1, w_ff2):
    B, T, _ = x.shape
    alpha = jax.nn.softmax(jnp.stack(attn_lses, axis=0), axis=0)
    o = jnp.einsum('gbth,gbthe->bthe', alpha, jnp.stack(attn_outs, axis=0)).astype(x.dtype)
    a = o.reshape(B, T, ATTN_WIDTH) @ w_attn_out
    c = causal_dwconv(u_ext, conv_w, conv_b)
    c = jax.nn.silu(layernorm(c, conv_ln_g, conv_ln_b)) @ w_conv_out
    g_a, g_c = jnp.split(gates, 2, axis=-1)
    merged = jax.nn.sigmoid(g_a) * a + jax.nn.sigmoid(g_c) * c
    h = x + merged @ w_o
    hn = rmsnorm(h, norm2_g)
    return h + jnp.square(jax.nn.relu(hn @ w_ff1)) @ w_ff2


def setup_inputs(seed: int = 0) -> dict:
    key = jax.random.key(seed)
    ks = jax.random.split(key, 24)
    f32 = jnp.float32
    nrm = lambda k, shape, scale: jax.random.normal(k, shape, f32) * scale
    inp = {}
    inp['x_prompt'] = nrm(ks[0], (BATCH, SEQ, D_MODEL), 1.0)
    inp['x_sample'] = nrm(ks[1], (DEC_BATCH, DEC_SEQ, D_MODEL), 1.0)
    for g in range(N_GROUPS):
        lb = min(GROUP_WINDOWS[g], PAST_LEN)
        shape = (DEPTH, DEC_BATCH, lb, HEADS_PER_GROUP, HEAD_DIM)
        inp['cache_k%d' % (g + 1)] = nrm(ks[2 + 2 * g], shape, 1.0)
        inp['cache_v%d' % (g + 1)] = nrm(ks[3 + 2 * g], shape, 1.0)
    inp['state_conv'] = nrm(ks[8], (DEPTH, DEC_BATCH, CONV_WIDTH - 1, CONV_CH), 0.5)
    inp['norm1_g'] = 1.0 + nrm(ks[9], (DEPTH, D_MODEL), 0.05)
    inp['w_in'] = nrm(ks[10], (DEPTH, D_MODEL, IN_WIDTH), D_MODEL ** -0.5)
    inp['q_norm_g'] = 1.0 + nrm(ks[11], (DEPTH, N_GROUPS, HEAD_DIM), 0.05)
    inp['k_norm_g'] = 1.0 + nrm(ks[12], (DEPTH, N_GROUPS, HEAD_DIM), 0.05)
    inp['conv_w'] = nrm(ks[13], (DEPTH, CONV_WIDTH, CONV_CH), CONV_WIDTH ** -0.5)
    inp['conv_b'] = nrm(ks[14], (DEPTH, CONV_CH), 0.01)
    inp['conv_ln_g'] = 1.0 + nrm(ks[15], (DEPTH, CONV_CH), 0.05)
    inp['conv_ln_b'] = nrm(ks[16], (DEPTH, CONV_CH), 0.01)
    inp['w_attn_out'] = nrm(ks[17], (DEPTH, ATTN_WIDTH, D_MODEL), ATTN_WIDTH ** -0.5)
    inp['w_conv_out'] = nrm(ks[18], (DEPTH, CONV_CH, D_MODEL), CONV_CH ** -0.5)
    inp['w_o'] = nrm(ks[19], (DEPTH, D_MODEL, D_MODEL), D_MODEL ** -0.5)
    inp['norm2_g'] = 1.0 + nrm(ks[20], (DEPTH, D_MODEL), 0.05)
    inp['w_ff1'] = nrm(ks[21], (DEPTH, D_MODEL, D_FF), D_MODEL ** -0.5)
    inp['w_ff2'] = nrm(ks[22], (DEPTH, D_FF, D_MODEL), D_FF ** -0.5)
    return inp


def reference(x_prompt, x_sample, cache_k1, cache_v1, cache_k2, cache_v2, cache_k3, cache_v3,
              state_conv, norm1_g, w_in, q_norm_g, k_norm_g, conv_w, conv_b, conv_ln_g, conv_ln_b,
              w_attn_out, w_conv_out, w_o, norm2_g, w_ff1, w_ff2):
    cache_k = (cache_k1, cache_k2, cache_k3)
    cache_v = (cache_v1, cache_v2, cache_v3)
    pk = [[] for _ in range(N_GROUPS)]
    pv = [[] for _ in range(N_GROUPS)]
    sk = [[] for _ in range(N_GROUPS)]
    sv = [[] for _ in range(N_GROUPS)]
    p_conv, s_conv = [], []
    xp, xs = x_prompt, x_sample
    for l in range(DEPTH):
        out_w = (conv_w[l], conv_b[l], conv_ln_g[l], conv_ln_b[l], w_attn_out[l], w_conv_out[l],
                 w_o[l], norm2_g[l], w_ff1[l], w_ff2[l])
        q, k, v, u, gates = mixer_inputs(xp, norm1_g[l], w_in[l], q_norm_g[l], k_norm_g[l])
        outs, lses = [], []
        for g in range(N_GROUPS):
            o, lse = dilated_attention_prompt(q[:, :, g], k[:, :, g], v[:, :, g],
                                              GROUP_DILATIONS[g], GROUP_WINDOWS[g])
            outs.append(o)
            lses.append(lse)
            keep = min(GROUP_WINDOWS[g], xp.shape[1])
            pk[g].append(k[:, -keep:, g])
            pv[g].append(v[:, -keep:, g])
        u_ext = jnp.pad(u, ((0, 0), (CONV_WIDTH - 1, 0), (0, 0)))
        p_conv.append(u_ext[:, -(CONV_WIDTH - 1):])
        xp = mixer_outputs(xp, outs, lses, u_ext, gates, *out_w)
        q, k, v, u, gates = mixer_inputs(xs, norm1_g[l], w_in[l], q_norm_g[l], k_norm_g[l])
        outs, lses = [], []
        for g in range(N_GROUPS):
            o, lse, nk, nv = dilated_attention_sample(q[:, :, g], k[:, :, g], v[:, :, g],
                                                      cache_k[g][l], cache_v[g][l],
                                                      GROUP_DILATIONS[g], GROUP_WINDOWS[g])
            outs.append(o)
            lses.append(lse)
            sk[g].append(nk)
            sv[g].append(nv)
        u_ext = jnp.concatenate([state_conv[l], u], axis=1)
        s_conv.append(u_ext[:, -(CONV_WIDTH - 1):])
        xs = mixer_outputs(xs, outs, lses, u_ext, gates, *out_w)
    st = lambda lst: jnp.stack(lst, axis=0)
    return (xp, xs,
            st(pk[0]), st(pv[0]), st(pk[1]), st(pv[1]), st(pk[2]), st(pv[2]), st(p_conv),
            st(sk[0]), st(sv[0]), st(sk[1]), st(sv[1]), st(sk[2]), st(sv[2]), st(s_conv))
```

```python
import functools

import jax
import jax.numpy as jnp
from jax import lax
from jax.experimental import pallas as pl
from jax.experimental.pallas import tpu as pltpu

F32, BF = jnp.float32, jnp.bfloat16

D_MODEL = 1024
N_GROUPS = 3
WINDOWS = (128, 512, 2048)
DILATIONS = (1, 4, 16)
N_HEADS = 8
HEAD_DIM = 64
ATTN_W = N_HEADS * HEAD_DIM
QKV_W = N_GROUPS * ATTN_W
CONV_CH = D_MODEL
CONV_TAPS = 31
D_FF = 4 * D_MODEL
IN_W = 3 * QKV_W + 2 * CONV_CH + 2 * D_MODEL
EPS = 1e-6
ATTN_SCALE = 1.0 / 8.0
BAND = 128
NEG = -1e30

LANES = 128
HALO = 32
PCH = 256
HEAD_CHUNK = 4
VMEM_LIMIT = 56 << 20


def _params(n_axes, vmem=VMEM_LIMIT):
    return pltpu.CompilerParams(dimension_semantics=("arbitrary",) * n_axes, vmem_limit_bytes=vmem)


def _resident(shape):
    nd = len(shape)
    return pl.BlockSpec(shape, lambda *_: (0,) * nd, pipeline_mode=pl.Buffered(1))


def _inproj_kernel(prompt, x_ref, g1_ref, w_ref, gq_ref, gk_ref, p_ref, *outs):
    if prompt:
        q_ref, k_ref, v_ref, u_ref, sg_ref, kt1, kt2, kt3, vt1, vt2, vt3, ut_ref = outs
        kts, vts = (kt1, kt2, kt3), (vt1, vt2, vt3)
    else:
        q_ref, k_ref, v_ref, u_ref, sg_ref = outs
    tm = x_ref.shape[0]
    x = x_ref[...]
    xn = (x * lax.rsqrt(jnp.mean(x * x, axis=-1, keepdims=True) + EPS) * g1_ref[...]).astype(BF)

    def proj(c0):
        return jnp.dot(xn, w_ref[:, c0:c0 + PCH], preferred_element_type=F32)

    def head_norm(t, gain):
        ms = jnp.dot((t * t).astype(BF), p_ref[...], preferred_element_type=F32)
        return t * lax.rsqrt(ms + EPS) * gain

    for c in range(QKV_W // PCH):
        cs = slice(c * PCH, (c + 1) * PCH)
        q_ref[:, cs] = head_norm(proj(c * PCH), gq_ref[:, cs]).astype(q_ref.dtype)
        kn = head_norm(proj(QKV_W + c * PCH), gk_ref[:, cs])
        k_ref[:, cs] = kn.astype(k_ref.dtype)
        vv = proj(2 * QKV_W + c * PCH)
        v_ref[:, cs] = vv.astype(v_ref.dtype)
        if prompt:
            g, off = divmod(c * PCH, ATTN_W)
            rows = kts[g].shape[0]
            kts[g][:, off:off + PCH] = kn[tm - rows:, :]
            vts[g][:, off:off + PCH] = vv[tm - rows:, :]
    for c in range(CONV_CH // PCH):
        cs = slice(c * PCH, (c + 1) * PCH)
        u = proj(3 * QKV_W + c * PCH) * jax.nn.sigmoid(proj(3 * QKV_W + CONV_CH + c * PCH))
        u_ref[:, cs] = u.astype(u_ref.dtype)
        if prompt:
            ut_ref[:, cs] = u[tm - HALO:, :]
    for c in range(2 * D_MODEL // PCH):
        cs = slice(c * PCH, (c + 1) * PCH)
        sg_ref[:, cs] = jax.nn.sigmoid(proj(3 * QKV_W + 2 * CONV_CH + c * PCH)).astype(sg_ref.dtype)


def _inproj_consts(norm1_g, w_in, q_norm_g, k_norm_g):
    gq = (jnp.repeat(q_norm_g, N_HEADS, axis=0) * ATTN_SCALE).reshape(1, QKV_W)
    gk = jnp.repeat(k_norm_g, N_HEADS, axis=0).reshape(1, QKV_W)
    head = jnp.arange(PCH) // HEAD_DIM
    pmat = jnp.where(head[:, None] == head[None, :], 1.0 / HEAD_DIM, 0.0).astype(BF)
    return norm1_g.reshape(1, D_MODEL), w_in.astype(BF), gq, gk, pmat


def _const_specs():
    return [_resident((1, D_MODEL)), _resident((D_MODEL, IN_W)), _resident((1, QKV_W)),
            _resident((1, QKV_W)), _resident((PCH, PCH))]


def _inproj_prompt(x, consts, tm):
    B, S, _ = x.shape
    nt = S // tm
    assert S % tm == 0 and S >= max(WINDOWS) and tm >= HALO
    row = lambda width: pl.BlockSpec((tm, width), lambda b, i: (b * nt + i, 0))

    def tail_spec(w):
        if w >= tm:
            assert w % tm == 0
            first = (S - w) // tm
            return pl.BlockSpec((None, tm, ATTN_W), lambda b, i: (b, jnp.maximum(i - first, 0), 0))
        assert tm % w == 0
        return pl.BlockSpec((None, w, ATTN_W), lambda b, i: (b, 0, 0))

    tails = [tail_spec(w) for w in WINDOWS]
    tail_shapes = [jax.ShapeDtypeStruct((B, w, ATTN_W), F32) for w in WINDOWS]
    M = B * S
    return pl.pallas_call(
        functools.partial(_inproj_kernel, True),
        grid=(B, nt),
        in_specs=[row(D_MODEL)] + _const_specs(),
        out_specs=[row(QKV_W), row(QKV_W), row(QKV_W), row(CONV_CH), row(2 * D_MODEL)] + tails + tails
                  + [pl.BlockSpec((None, HALO, CONV_CH), lambda b, i: (b, 0, 0))],
        out_shape=[jax.ShapeDtypeStruct((M, QKV_W), BF)] * 3
                  + [jax.ShapeDtypeStruct((M, CONV_CH), BF), jax.ShapeDtypeStruct((M, 2 * D_MODEL), BF)]
                  + tail_shapes + tail_shapes + [jax.ShapeDtypeStruct((B, HALO, CONV_CH), F32)],
        compiler_params=_params(2),
        name="inproj_prompt",
    )(x.reshape(M, D_MODEL), *consts)


def _inproj_sample(x2d, consts):
    M = x2d.shape[0]
    full = lambda width: pl.BlockSpec((M, width), lambda i: (0, 0))
    return pl.pallas_call(
        functools.partial(_inproj_kernel, False),
        grid=(1,),
        in_specs=[full(D_MODEL)] + _const_specs(),
        out_specs=[full(QKV_W), full(QKV_W), full(QKV_W), full(CONV_CH), full(2 * D_MODEL)],
        out_shape=[jax.ShapeDtypeStruct((M, QKV_W), F32)] * 3
                  + [jax.ShapeDtypeStruct((M, CONV_CH), F32), jax.ShapeDtypeStruct((M, 2 * D_MODEL), BF)],
        compiler_params=_params(1),
        name="inproj_sample",
    )(x2d, *consts)


def _attn_kernel(q_ref, kc_ref, kp_ref, vc_ref, vp_ref, o_ref, lse_ref):
    tq = q_ref.shape[0]
    not_first = pl.program_id(2) > 0
    row = lax.broadcasted_iota(jnp.int32, (BAND, BAND), 0)
    col = lax.broadcasted_iota(jnp.int32, (BAND, BAND), 1)
    lane = lax.broadcasted_iota(jnp.int32, (BAND, LANES), 1)
    nt = (((1,), (1,)), ((), ()))
    for sb in range(tq // BAND):
        cur = slice(sb * BAND, (sb + 1) * BAND)
        q = q_ref[cur, :]
        if sb == 0:
            k_a, v_a = kp_ref[...], vp_ref[...]
            mask_a = jnp.logical_and(col >= row, not_first)
        else:
            prev = slice((sb - 1) * BAND, sb * BAND)
            k_a, v_a = kc_ref[prev, :], vc_ref[prev, :]
            mask_a = col >= row
        k_b, v_b = kc_ref[cur, :], vc_ref[cur, :]
        mask_b = col <= row
        outs = []
        lse_t = jnp.zeros((BAND, LANES), F32)
        for h in range(N_HEADS):
            hs = slice(h * HEAD_DIM, (h + 1) * HEAD_DIM)
            qh = q[:, hs]
            s_a = jnp.where(mask_a, lax.dot_general(qh, k_a[:, hs], nt, preferred_element_type=F32), NEG)
            s_b = jnp.where(mask_b, lax.dot_general(qh, k_b[:, hs], nt, preferred_element_type=F32), NEG)
            m = jnp.maximum(jnp.max(s_a, axis=-1, keepdims=True), jnp.max(s_b, axis=-1, keepdims=True))
            p_a = jnp.exp(s_a - m)
            p_b = jnp.exp(s_b - m)
            l = jnp.sum(p_a, axis=-1, keepdims=True) + jnp.sum(p_b, axis=-1, keepdims=True)
            acc = (jnp.dot(p_a.astype(BF), v_a[:, hs], preferred_element_type=F32)
                   + jnp.dot(p_b.astype(BF), v_b[:, hs], preferred_element_type=F32))
            outs.append(acc / l)
            lse_t = jnp.where(lane == h, m + jnp.log(l), lse_t)
        o_ref[cur, :] = jnp.concatenate(outs, axis=-1).astype(o_ref.dtype)
        lse_ref[cur, :] = lse_t


def _attention_prompt(q, k, v, B, S, g):
    d = DILATIONS[g]
    assert WINDOWS[g] // d == BAND and S % (d * BAND) == 0
    L = S // d
    tq = min(512, L)
    assert L % tq == 0
    sub = tq // BAND
    view = lambda t: t.reshape(B, L, d * QKV_W)
    cur = pl.BlockSpec((None, tq, ATTN_W), lambda b, r, j: (b, j, r * N_GROUPS + g))
    prev = pl.BlockSpec((None, BAND, ATTN_W),
                        lambda b, r, j: (b, jnp.maximum(j * sub - 1, 0), r * N_GROUPS + g))
    o, lse = pl.pallas_call(
        _attn_kernel,
        grid=(B, d, L // tq),
        in_specs=[cur, cur, prev, cur, prev],
        out_specs=[pl.BlockSpec((None, tq, ATTN_W), lambda b, r, j: (b, j, r)),
                   pl.BlockSpec((None, tq, LANES), lambda b, r, j: (b, j, r))],
        out_shape=[jax.ShapeDtypeStruct((B, L, d * ATTN_W), BF),
                   jax.ShapeDtypeStruct((B, L, d * LANES), F32)],
        compiler_params=_params(3),
        name="attn_g%d" % (g + 1),
    )(view(q), view(k), view(k), view(v), view(v))
    return o.reshape(B * S, ATTN_W), lse.reshape(B * S, LANES)


def _ln_swish(c, g, b):
    mu = jnp.mean(c, axis=-1, keepdims=True)
    cc = c - mu
    var = jnp.mean(cc * cc, axis=-1, keepdims=True)
    y = cc * lax.rsqrt(var + EPS) * g + b
    return y * jax.nn.sigmoid(y)


def _conv_kernel(u_ref, halo_ref, w_ref, b_ref, lg_ref, lb_ref, o_ref, ext_ref, cv_ref):
    tc = u_ref.shape[0]
    rc = 64
    ext_ref[0:HALO, :] = jnp.where(pl.program_id(1) > 0, halo_ref[...].astype(F32), 0.0)
    ext_ref[HALO:HALO + tc, :] = u_ref[...].astype(F32)
    first = HALO - (CONV_TAPS - 1)
    for r0 in range(0, tc, rc):
        for l0 in range(0, CONV_CH, LANES):
            ls = slice(l0, l0 + LANES)
            acc = jnp.broadcast_to(b_ref[:, ls], (rc, LANES))
            for j in range(CONV_TAPS):
                acc = acc + w_ref[j:j + 1, ls] * ext_ref[r0 + first + j:r0 + first + j + rc, ls]
            cv_ref[r0:r0 + rc, ls] = acc
    for r0 in range(0, tc, rc):
        o_ref[r0:r0 + rc, :] = _ln_swish(cv_ref[r0:r0 + rc, :], lg_ref[...], lb_ref[...]).astype(o_ref.dtype)


def _conv_prompt(u, B, S, conv_w, conv_b, ln_g, ln_b, tc):
    nt = S // tc
    assert S % tc == 0 and tc % HALO == 0
    hb = tc // HALO
    vec = _resident((1, CONV_CH))
    return pl.pallas_call(
        _conv_kernel,
        grid=(B, nt),
        in_specs=[pl.BlockSpec((tc, CONV_CH), lambda b, i: (b * nt + i, 0)),
                  pl.BlockSpec((HALO, CONV_CH), lambda b, i: (jnp.maximum((b * nt + i) * hb - 1, 0), 0)),
                  _resident((CONV_TAPS, CONV_CH)), vec, vec, vec],
        out_specs=pl.BlockSpec((tc, CONV_CH), lambda b, i: (b * nt + i, 0)),
        out_shape=jax.ShapeDtypeStruct((B * S, CONV_CH), BF),
        scratch_shapes=[pltpu.VMEM((tc + HALO, CONV_CH), F32), pltpu.VMEM((tc, CONV_CH), F32)],
        compiler_params=_params(2),
        name="conv_prompt",
    )(u, u, conv_w, conv_b, ln_g, ln_b)


def _sample_conv_kernel(st_ref, un_ref, w_ref, b_ref, lg_ref, lb_ref, ca_ref, so_ref):
    n = CONV_TAPS - 1
    un = un_ref[...]
    acc = b_ref[...] + w_ref[n:n + 1, :] * un
    for j in range(n):
        acc = acc + w_ref[j:j + 1, :] * st_ref[j]
    ca_ref[...] = _ln_swish(acc, lg_ref[...], lb_ref[...]).astype(ca_ref.dtype)
    so_ref[0:n - 1] = st_ref[1:n]
    so_ref[n - 1] = un


def _conv_sample(state, u_new, conv_w, conv_b, ln_g, ln_b, bb):
    n, nb, _ = state.shape
    assert n == CONV_TAPS - 1 and nb % bb == 0
    vec = _resident((1, CONV_CH))
    return pl.pallas_call(
        _sample_conv_kernel,
        grid=(nb // bb,),
        in_specs=[pl.BlockSpec((n, bb, CONV_CH), lambda i: (0, i, 0)),
                  pl.BlockSpec((bb, CONV_CH), lambda i: (i, 0)),
                  _resident((CONV_TAPS, CONV_CH)), vec, vec, vec],
        out_specs=[pl.BlockSpec((bb, CONV_CH), lambda i: (i, 0)),
                   pl.BlockSpec((n, bb, CONV_CH), lambda i: (0, i, 0))],
        out_shape=[jax.ShapeDtypeStruct((nb, CONV_CH), BF), jax.ShapeDtypeStruct(state.shape, F32)],
        compiler_params=_params(1),
        name="conv_sample",
    )(state, u_new, conv_w, conv_b, ln_g, ln_b)


def _sample_attn_kernel(qt_ref, knt_ref, vnt_ref, k1, v1, k2, v2, k3, v3, o_ref, ok1, ov1, ok2, ov2, ok3, ov3):
    kin, vin = (k1, k2, k3), (v1, v2, v3)
    kout, vout = (ok1, ok2, ok3), (ov1, ov2, ov3)
    o_tile = jnp.zeros((HEAD_DIM, HEAD_CHUNK), F32)
    o_lane = lax.broadcasted_iota(jnp.int32, (HEAD_DIM, HEAD_CHUNK), 1)
    for hh in range(HEAD_CHUNK):
        rows = slice(hh * HEAD_DIM, (hh + 1) * HEAD_DIM)
        scores, new_scores, m = [], [], None
        for g in range(N_GROUPS):
            lb = kin[g].shape[1]
            cl = g * HEAD_CHUNK + hh
            qc = qt_ref[:, cl:cl + 1]
            s = jnp.sum(kin[g][rows, :] * qc, axis=0, keepdims=True)
            if DILATIONS[g] > 1:
                pos = lax.broadcasted_iota(jnp.int32, (1, lb), 1)
                s = jnp.where((pos & (DILATIONS[g] - 1)) == 0, s, NEG)
            sn = jnp.sum(knt_ref[:, cl:cl + 1] * qc, axis=0, keepdims=True)
            mg = jnp.maximum(jnp.max(s, axis=1, keepdims=True), sn)
            m = mg if m is None else jnp.maximum(m, mg)
            scores.append(s)
            new_scores.append(sn)
        l = jnp.zeros((1, 1), F32)
        acc = jnp.zeros((HEAD_DIM, 1), F32)
        for g in range(N_GROUPS):
            cl = g * HEAD_CHUNK + hh
            p = jnp.exp(scores[g] - m)
            pn = jnp.exp(new_scores[g] - m)
            l = l + jnp.sum(p, axis=1, keepdims=True) + pn
            acc = acc + jnp.sum(vin[g][rows, :] * p, axis=1, keepdims=True) + vnt_ref[:, cl:cl + 1] * pn
        o_tile = jnp.where(o_lane == hh, acc / l, o_tile)
        for g in range(N_GROUPS):
            lb = kin[g].shape[1]
            cl = g * HEAD_CHUNK + hh
            last = lax.broadcasted_iota(jnp.int32, (HEAD_DIM, lb), 1) == lb - 1
            kout[g][rows, :] = jnp.where(last, knt_ref[:, cl:cl + 1], pltpu.roll(kin[g][rows, :], lb - 1, 1))
            vout[g][rows, :] = jnp.where(last, vnt_ref[:, cl:cl + 1], pltpu.roll(vin[g][rows, :], lb - 1, 1))
    o_ref[...] = o_tile


def _columns(t):
    n = t.shape[0]
    t = t.reshape(n, N_GROUPS, N_HEADS // HEAD_CHUNK, HEAD_CHUNK, HEAD_DIM)
    return t.transpose(0, 2, 4, 1, 3).reshape(n, N_HEADS // HEAD_CHUNK, HEAD_DIM, N_GROUPS * HEAD_CHUNK)


def _attention_sample(q, k_new, v_new, caches):
    nb = q.shape[0]
    nch = N_HEADS // HEAD_CHUNK
    rows = HEAD_CHUNK * HEAD_DIM
    for g, c in enumerate(caches):
        assert c.shape == (nb, ATTN_W, WINDOWS[g // 2]), "window buffers are expected full"
    col_spec = pl.BlockSpec((None, None, HEAD_DIM, N_GROUPS * HEAD_CHUNK), lambda b, h: (b, h, 0, 0))
    buf_specs = [pl.BlockSpec((None, rows, c.shape[2]), lambda b, h: (b, h, 0)) for c in caches]
    outs = pl.pallas_call(
        _sample_attn_kernel,
        grid=(nb, nch),
        in_specs=[col_spec] * 3 + buf_specs,
        out_specs=[pl.BlockSpec((None, None, HEAD_DIM, HEAD_CHUNK), lambda b, h: (b, h, 0, 0))] + buf_specs,
        out_shape=[jax.ShapeDtypeStruct((nb, nch, HEAD_DIM, HEAD_CHUNK), F32)]
                  + [jax.ShapeDtypeStruct(c.shape, F32) for c in caches],
        compiler_params=_params(2),
        name="sample_attn",
    )(_columns(q), _columns(k_new), _columns(v_new), *caches)
    o = outs[0].transpose(0, 1, 3, 2).reshape(nb, ATTN_W)
    return o, outs[1:]


def _main_kernel(combine, x_ref, *refs):
    if combine:
        o_refs, l_refs, e_ref = refs[0:3], refs[3:6], refs[6]
        refs = refs[7:]
    else:
        o_ref = refs[0]
        refs = refs[1:]
    ca_ref, sg_ref, wa_ref, wc_ref, wo_ref, g2_ref, w1_ref, w2_ref, y_ref, hid_ref = refs
    if combine:
        ls = [r[...] for r in l_refs]
        m = jnp.maximum(jnp.maximum(ls[0], ls[1]), ls[2])
        es = [jnp.exp(t - m) for t in ls]
        inv = 1.0 / (es[0] + es[1] + es[2])
        o = None
        for g in range(N_GROUPS):
            alpha = es[g] * inv
            hi = alpha.astype(BF)
            lo = (alpha - hi.astype(F32)).astype(BF)
            wide = (jnp.dot(hi, e_ref[...], preferred_element_type=F32)
                    + jnp.dot(lo, e_ref[...], preferred_element_type=F32))
            t = wide * o_refs[g][...].astype(F32)
            o = t if o is None else o + t
    else:
        o = o_ref[...]
    a = jnp.dot(o.astype(BF), wa_ref[...], preferred_element_type=F32)
    c = jnp.dot(ca_ref[...], wc_ref[...], preferred_element_type=F32)
    merged = sg_ref[:, :D_MODEL].astype(F32) * a + sg_ref[:, D_MODEL:].astype(F32) * c
    h = x_ref[...] + jnp.dot(merged.astype(BF), wo_ref[...], preferred_element_type=F32)
    hn = (h * lax.rsqrt(jnp.mean(h * h, axis=-1, keepdims=True) + EPS) * g2_ref[...]).astype(BF)
    fch = 512
    for c0 in range(0, D_FF, fch):
        t = jnp.maximum(jnp.dot(hn, w1_ref[:, c0:c0 + fch], preferred_element_type=F32), 0.0)
        hid_ref[:, c0:c0 + fch] = (t * t).astype(BF)
    y_ref[...] = h + jnp.dot(hid_ref[...], w2_ref[...], preferred_element_type=F32)


def _main(x2d, attn, ca, sg, weights, tm):
    M = x2d.shape[0]
    assert M % tm == 0
    row = lambda width: pl.BlockSpec((tm, width), lambda i: (i, 0))
    combine = isinstance(attn, tuple)
    if combine:
        os_, ls_ = attn
        head = jnp.arange(ATTN_W) // HEAD_DIM
        spread = (jnp.arange(LANES)[:, None] == head[None, :]).astype(BF)
        attn_args = list(os_) + list(ls_) + [spread]
        attn_specs = [row(ATTN_W)] * 3 + [row(LANES)] * 3 + [_resident((LANES, ATTN_W))]
    else:
        attn_args, attn_specs = [attn], [row(ATTN_W)]
    wa, wc, wo, g2, w1, w2 = weights
    return pl.pallas_call(
        functools.partial(_main_kernel, combine),
        grid=(M // tm,),
        in_specs=[row(D_MODEL)] + attn_specs + [row(CONV_CH), row(2 * D_MODEL)]
                 + [_resident(w.shape) for w in weights],
        out_specs=row(D_MODEL),
        out_shape=jax.ShapeDtypeStruct((M, D_MODEL), F32),
        scratch_shapes=[pltpu.VMEM((tm, D_FF), BF)],
        compiler_params=_params(1),
        name="main_prompt" if combine else "main_sample",
    )(x2d, *attn_args, ca, sg, *weights)


def kernel(x_prompt, x_sample, cache_k1, cache_v1, cache_k2, cache_v2, cache_k3, cache_v3, state_conv,
           norm1_g, w_in, q_norm_g, k_norm_g, conv_w, conv_b, conv_ln_g, conv_ln_b,
           w_attn_out, w_conv_out, w_o, norm2_g, w_ff1, w_ff2):
    assert w_in.shape[0] == 1, "one layer"
    B, S, _ = x_prompt.shape
    nb, ns, _ = x_sample.shape
    assert ns == 1, "one new token per sample row"
    consts = _inproj_consts(norm1_g[0], w_in[0], q_norm_g[0], k_norm_g[0])
    conv_args = (conv_w[0], conv_b, conv_ln_g, conv_ln_b)
    weights = (w_attn_out[0].astype(BF), w_conv_out[0].astype(BF), w_o[0].astype(BF), norm2_g,
               w_ff1[0].astype(BF), w_ff2[0].astype(BF))

    (q, k, v, u, sg, kt1, kt2, kt3, vt1, vt2, vt3, ut) = _inproj_prompt(x_prompt, consts, tm=256)
    attn = [_attention_prompt(q, k, v, B, S, g) for g in range(N_GROUPS)]
    ca = _conv_prompt(u, B, S, *conv_args, tc=256)
    y_prompt = _main(x_prompt.reshape(B * S, D_MODEL), (tuple(a[0] for a in attn), tuple(a[1] for a in attn)),
                     ca, sg, weights, tm=256).reshape(B, S, D_MODEL)
    tail = lambda t: t.reshape(1, B, t.shape[1], N_HEADS, HEAD_DIM)
    p_conv = ut[None, :, HALO - (CONV_TAPS - 1):, :]

    qs, ks, vs, us, sgs = _inproj_sample(x_sample.reshape(nb, D_MODEL), consts)
    to_buf = lambda c: c[0].transpose(0, 2, 3, 1).reshape(nb, ATTN_W, c.shape[2])
    from_buf = lambda c: c.reshape(nb, N_HEADS, HEAD_DIM, c.shape[2]).transpose(0, 3, 1, 2)[None]
    o_s, bufs = _attention_sample(qs, ks, vs, [to_buf(c) for c in
                                               (cache_k1, cache_v1, cache_k2, cache_v2, cache_k3, cache_v3)])
    ca_s, state_new = _conv_sample(state_conv[0].transpose(1, 0, 2), us, *conv_args, bb=min(32, nb))
    y_sample = _main(x_sample.reshape(nb, D_MODEL), o_s, ca_s, sgs, weights, tm=nb).reshape(nb, 1, D_MODEL)
    s_conv = state_new.transpose(1, 0, 2)[None]

    return (y_prompt, y_sample, tail(kt1), tail(vt1), tail(kt2), tail(vt2), tail(kt3), tail(vt3), p_conv,
            *(from_buf(b) for b in bufs), s_conv)
```

```python
import functools
import math

import jax
import jax.numpy as jnp
from jax import lax
from jax.experimental import pallas as pl
from jax.experimental.pallas import tpu as pltpu

F32, BF = jnp.float32, jnp.bfloat16

D_MODEL = 1024
N_GROUPS = 3
WINDOWS = (128, 512, 2048)
DILATIONS = (1, 4, 16)
N_HEADS = 8
HEAD_DIM = 64
ATTN_W = N_HEADS * HEAD_DIM
QKV_W = N_GROUPS * ATTN_W
CONV_CH = D_MODEL
CONV_TAPS = 31
D_FF = 4 * D_MODEL
IN_W = 3 * QKV_W + 2 * CONV_CH + 2 * D_MODEL
EPS = 1e-6
LOG2E = math.log2(math.e)
LN2 = math.log(2.0)
Q_SCALE = LOG2E / 8.0
BAND = 128
NEG = -1e30

LANES = 128
SUBLANES = 8
PAIR = 2 * HEAD_DIM
HALO = 32
PCH = 256
HEAD_CHUNK = 4
VMEM_LIMIT = 56 << 20


def _params(n_axes, vmem=VMEM_LIMIT):
    return pltpu.CompilerParams(dimension_semantics=("arbitrary",) * n_axes, vmem_limit_bytes=vmem)


def _resident(shape):
    nd = len(shape)
    return pl.BlockSpec(shape, lambda *_: (0,) * nd, pipeline_mode=pl.Buffered(1))


def _inproj_kernel(prompt, x_ref, g1_ref, w_ref, gq_ref, gk_ref, p_ref, *refs):
    if prompt:
        qs, ks, vs = refs[0:3], refs[3:6], refs[6:9]
        u_ref, sg_ref = refs[9:11]
        kts, vts = refs[11:14], refs[14:17]
        ut_ref, stage_ref = refs[17:19]
    else:
        q_ref, k_ref, v_ref, u_ref, sg_ref = refs
    tm = x_ref.shape[0]
    x = x_ref[...]
    xn = (x * lax.rsqrt(jnp.mean(x * x, axis=-1, keepdims=True) + EPS) * g1_ref[...]).astype(BF)

    def proj(c0):
        return jnp.dot(xn, w_ref[:, c0:c0 + PCH], preferred_element_type=F32)

    def head_norm(t, gain):
        ms = jnp.dot((t * t).astype(BF), p_ref[...], preferred_element_type=F32)
        return t * lax.rsqrt(ms + EPS) * gain

    slot = [0]

    def put(dst, g, off, t):
        d = DILATIONS[g]
        if d == 1:
            dst[:, off:off + PCH] = t.astype(dst.dtype)
            return
        base = slot[0]
        slot[0] += PCH // LANES
        for c in range(PCH // LANES):
            stage_ref[base + c] = t[:, c * LANES:(c + 1) * LANES]
        for r in range(d):
            parts = [stage_ref[base + c, pl.ds(r, tm // d, stride=d), :] for c in range(PCH // LANES)]
            dst[r, :, off:off + PCH] = jnp.concatenate(parts, axis=-1).astype(dst.dtype)

    for c in range(QKV_W // PCH):
        cs = slice(c * PCH, (c + 1) * PCH)
        g, off = divmod(c * PCH, ATTN_W)
        qn = head_norm(proj(c * PCH), gq_ref[:, cs])
        kn = head_norm(proj(QKV_W + c * PCH), gk_ref[:, cs])
        vv = proj(2 * QKV_W + c * PCH)
        if prompt:
            put(qs[g], g, off, qn)
            put(ks[g], g, off, kn)
            put(vs[g], g, off, vv)
            rows = kts[g].shape[0]
            kts[g][:, off:off + PCH] = kn[tm - rows:, :]
            vts[g][:, off:off + PCH] = vv[tm - rows:, :]
        else:
            q_ref[:, cs] = qn
            k_ref[:, cs] = kn
            v_ref[:, cs] = vv
    for c in range(CONV_CH // PCH):
        cs = slice(c * PCH, (c + 1) * PCH)
        u = proj(3 * QKV_W + c * PCH) * jax.nn.sigmoid(proj(3 * QKV_W + CONV_CH + c * PCH))
        u_ref[:, cs] = u.astype(u_ref.dtype)
        if prompt:
            ut_ref[:, cs] = u[tm - HALO:, :]
    for c in range(2 * D_MODEL // PCH):
        cs = slice(c * PCH, (c + 1) * PCH)
        sg_ref[:, cs] = jax.nn.sigmoid(proj(3 * QKV_W + 2 * CONV_CH + c * PCH)).astype(sg_ref.dtype)


def _inproj_consts(norm1_g, w_in, q_norm_g, k_norm_g):
    gq = (jnp.repeat(q_norm_g, N_HEADS, axis=0) * Q_SCALE).reshape(1, QKV_W)
    gk = jnp.repeat(k_norm_g, N_HEADS, axis=0).reshape(1, QKV_W)
    head = jnp.arange(PCH) // HEAD_DIM
    pmat = jnp.where(head[:, None] == head[None, :], 1.0 / HEAD_DIM, 0.0).astype(BF)
    return norm1_g.reshape(1, D_MODEL), w_in.astype(BF), gq, gk, pmat


def _const_specs():
    return [_resident((1, D_MODEL)), _resident((D_MODEL, IN_W)), _resident((1, QKV_W)),
            _resident((1, QKV_W)), _resident((PCH, PCH))]


def _inproj_prompt(x, consts, tm):
    B, S, _ = x.shape
    nt = S // tm
    assert S % tm == 0 and S >= max(WINDOWS) and tm >= HALO
    assert all(tm % (2 * SUBLANES * d) == 0 for d in DILATIONS), "a residue's rows must fill bf16 tiles"
    row = lambda width: pl.BlockSpec((tm, width), lambda b, i: (b * nt + i, 0))

    def group_spec(d):
        if d == 1:
            return row(ATTN_W)
        return pl.BlockSpec((None, d, tm // d, ATTN_W), lambda b, i: (b, 0, i, 0))

    def group_shape(d):
        return jax.ShapeDtypeStruct((B * S, ATTN_W) if d == 1 else (B, d, S // d, ATTN_W), BF)

    def tail_spec(w):
        if w >= tm:
            assert w % tm == 0
            first = (S - w) // tm
            return pl.BlockSpec((None, tm, ATTN_W), lambda b, i: (b, jnp.maximum(i - first, 0), 0))
        assert tm % w == 0
        return pl.BlockSpec((None, w, ATTN_W), lambda b, i: (b, 0, 0))

    groups = [group_spec(d) for d in DILATIONS]
    group_shapes = [group_shape(d) for d in DILATIONS]
    tails = [tail_spec(w) for w in WINDOWS]
    tail_shapes = [jax.ShapeDtypeStruct((B, w, ATTN_W), F32) for w in WINDOWS]
    n_stage = 3 * sum(ATTN_W // LANES for d in DILATIONS if d > 1)
    M = B * S
    return pl.pallas_call(
        functools.partial(_inproj_kernel, True),
        grid=(B, nt),
        in_specs=[row(D_MODEL)] + _const_specs(),
        out_specs=groups * 3 + [row(CONV_CH), row(2 * D_MODEL)] + tails + tails
                  + [pl.BlockSpec((None, HALO, CONV_CH), lambda b, i: (b, 0, 0))],
        out_shape=group_shapes * 3
                  + [jax.ShapeDtypeStruct((M, CONV_CH), BF), jax.ShapeDtypeStruct((M, 2 * D_MODEL), BF)]
                  + tail_shapes + tail_shapes + [jax.ShapeDtypeStruct((B, HALO, CONV_CH), F32)],
        scratch_shapes=[pltpu.VMEM((n_stage, tm, LANES), F32)],
        compiler_params=_params(2),
        name="inproj_prompt",
    )(x.reshape(M, D_MODEL), *consts)


def _inproj_sample(x2d, consts):
    M = x2d.shape[0]
    full = lambda width: pl.BlockSpec((M, width), lambda i: (0, 0))
    return pl.pallas_call(
        functools.partial(_inproj_kernel, False),
        grid=(1,),
        in_specs=[full(D_MODEL)] + _const_specs(),
        out_specs=[full(QKV_W), full(QKV_W), full(QKV_W), full(CONV_CH), full(2 * D_MODEL)],
        out_shape=[jax.ShapeDtypeStruct((M, QKV_W), F32)] * 3
                  + [jax.ShapeDtypeStruct((M, CONV_CH), F32), jax.ShapeDtypeStruct((M, 2 * D_MODEL), BF)],
        compiler_params=_params(1),
        name="inproj_sample",
    )(x2d, *consts)


def _attn_kernel(q_ref, kc_ref, kp_ref, vc_ref, vp_ref, o_ref, lse_ref, kx_ref, vx_ref):
    tq = q_ref.shape[0]
    not_first = pl.program_id(2) > 0
    kt = jnp.concatenate([kp_ref[...].T, kc_ref[...].T], axis=1)
    zero = jnp.zeros((HEAD_DIM, tq + BAND), BF)
    for h in range(N_HEADS):
        kh = kt[h * HEAD_DIM:(h + 1) * HEAD_DIM]
        lo, hi = (kh, zero) if h % 2 == 0 else (zero, kh)
        kx_ref[h * PAIR:h * PAIR + HEAD_DIM, :] = lo
        kx_ref[h * PAIR + HEAD_DIM:(h + 1) * PAIR, :] = hi
    ones = jnp.ones((tq + BAND, PAIR), BF)
    for pr in range(N_HEADS // 2):
        vx_ref[0:BAND, 2 * pr * PAIR:(2 * pr + 1) * PAIR] = vp_ref[:, pr * PAIR:(pr + 1) * PAIR]
        vx_ref[BAND:, 2 * pr * PAIR:(2 * pr + 1) * PAIR] = vc_ref[:, pr * PAIR:(pr + 1) * PAIR]
        vx_ref[:, (2 * pr + 1) * PAIR:(2 * pr + 2) * PAIR] = ones
    a = lax.broadcasted_iota(jnp.int32, (BAND, 2 * BAND), 0)
    c = lax.broadcasted_iota(jnp.int32, (BAND, 2 * BAND), 1)
    band = jnp.logical_and(c >= a, c <= a + BAND)
    band0 = jnp.logical_and(band, jnp.logical_or(c >= BAND, not_first))
    lane = lax.broadcasted_iota(jnp.int32, (BAND, LANES), 1)
    for sb in range(tq // BAND):
        rows = slice(sb * BAND, (sb + 1) * BAND)
        keys = slice(sb * BAND, (sb + 2) * BAND)
        mask = band0 if sb == 0 else band
        q = q_ref[rows, :]
        m_t = jnp.zeros((BAND, LANES), F32)
        l_t = jnp.ones((BAND, LANES), F32)
        for pr in range(N_HEADS // 2):
            vv = vx_ref[keys, 2 * pr * PAIR:(2 * pr + 2) * PAIR]
            pair = None
            for hh in range(2):
                h = 2 * pr + hh
                s = jnp.dot(q[:, pr * PAIR:(pr + 1) * PAIR], kx_ref[h * PAIR:(h + 1) * PAIR, keys],
                            preferred_element_type=F32)
                s = jnp.where(mask, s, NEG)
                m = jnp.max(s, axis=-1, keepdims=True)
                p = jnp.exp2(s - m)
                r = jnp.dot(p.astype(BF), vv, preferred_element_type=F32)
                l = r[:, PAIR:]
                contrib = r[:, :PAIR] / l
                pair = contrib if hh == 0 else jnp.where(lane < HEAD_DIM, pair, contrib)
                m_t = jnp.where(lane == h, m, m_t)
                l_t = jnp.where(lane == h, l, l_t)
            o_ref[rows, pr * PAIR:(pr + 1) * PAIR] = pair.astype(o_ref.dtype)
        lse_ref[rows, :] = m_t * LN2 + jnp.log(l_t)


def _attention_prompt(q, k, v, g):
    B, d, L, _ = q.shape
    assert WINDOWS[g] // d == BAND and L % BAND == 0
    tq = min(512, L)
    assert L % tq == 0
    sub = tq // BAND
    cur = pl.BlockSpec((None, None, tq, ATTN_W), lambda b, r, j: (b, r, j, 0))
    prev = pl.BlockSpec((None, None, BAND, ATTN_W), lambda b, r, j: (b, r, jnp.maximum(j * sub - 1, 0), 0))
    return pl.pallas_call(
        _attn_kernel,
        grid=(B, d, L // tq),
        in_specs=[cur, cur, prev, cur, prev],
        out_specs=[cur, pl.BlockSpec((None, None, tq, LANES), lambda b, r, j: (b, r, j, 0))],
        out_shape=[jax.ShapeDtypeStruct((B, d, L, ATTN_W), BF), jax.ShapeDtypeStruct((B, d, L, LANES), F32)],
        scratch_shapes=[pltpu.VMEM((N_HEADS * PAIR, tq + BAND), BF), pltpu.VMEM((tq + BAND, 2 * ATTN_W), BF)],
        compiler_params=_params(3),
        name="attn_g%d" % (g + 1),
    )(q, k, k, v, v)


def _ln_swish(c, g, b):
    mu = jnp.mean(c, axis=-1, keepdims=True)
    cc = c - mu
    var = jnp.mean(cc * cc, axis=-1, keepdims=True)
    y = cc * lax.rsqrt(var + EPS) * g + b
    return y * jax.nn.sigmoid(y)


def _conv_kernel(u_ref, halo_ref, w_ref, b_ref, lg_ref, lb_ref, o_ref, ext_ref, cv_ref):
    tc = u_ref.shape[0]
    rc = 64
    n = tc + HALO
    ext_ref[0, 0:HALO, :] = jnp.where(pl.program_id(1) > 0, halo_ref[...].astype(F32), 0.0)
    ext_ref[0, HALO:n, :] = u_ref[...].astype(F32)
    for s in range(1, SUBLANES):
        ext_ref[s, 0:n - SUBLANES, :] = ext_ref[0, s:n - SUBLANES + s, :]
    lead = HALO - (CONV_TAPS - 1)

    def row_chunk(i, carry):
        r0 = pl.multiple_of(i * rc, rc)
        for l0 in range(0, CONV_CH, LANES):
            ls = slice(l0, l0 + LANES)
            acc = jnp.broadcast_to(b_ref[:, ls], (rc, LANES))
            taps = w_ref[:, ls]
            for j in range(CONV_TAPS):
                a8, s = divmod(lead + j, SUBLANES)
                acc = acc + taps[j:j + 1, :] * ext_ref[s, pl.ds(r0 + a8 * SUBLANES, rc), ls]
            cv_ref[pl.ds(r0, rc), ls] = acc
        o_ref[pl.ds(r0, rc), :] = _ln_swish(cv_ref[pl.ds(r0, rc), :], lg_ref[...], lb_ref[...]).astype(o_ref.dtype)
        return carry

    lax.fori_loop(0, tc // rc, row_chunk, 0)


def _conv_prompt(u, B, S, conv_w, conv_b, ln_g, ln_b, tc):
    nt = S // tc
    assert S % tc == 0 and tc % HALO == 0
    hb = tc // HALO
    vec = _resident((1, CONV_CH))
    return pl.pallas_call(
        _conv_kernel,
        grid=(B, nt),
        in_specs=[pl.BlockSpec((tc, CONV_CH), lambda b, i: (b * nt + i, 0)),
                  pl.BlockSpec((HALO, CONV_CH), lambda b, i: (jnp.maximum((b * nt + i) * hb - 1, 0), 0)),
                  _resident((CONV_TAPS, CONV_CH)), vec, vec, vec],
        out_specs=pl.BlockSpec((tc, CONV_CH), lambda b, i: (b * nt + i, 0)),
        out_shape=jax.ShapeDtypeStruct((B * S, CONV_CH), BF),
        scratch_shapes=[pltpu.VMEM((SUBLANES, tc + HALO, CONV_CH), F32), pltpu.VMEM((tc, CONV_CH), F32)],
        compiler_params=_params(2),
        name="conv_prompt",
    )(u, u, conv_w, conv_b, ln_g, ln_b)


def _sample_conv_kernel(st_ref, un_ref, w_ref, b_ref, lg_ref, lb_ref, ca_ref, so_ref):
    n = CONV_TAPS - 1
    un = un_ref[...]
    acc = b_ref[...] + w_ref[n:n + 1, :] * un
    for j in range(n):
        acc = acc + w_ref[j:j + 1, :] * st_ref[j]
    ca_ref[...] = _ln_swish(acc, lg_ref[...], lb_ref[...]).astype(ca_ref.dtype)
    so_ref[0:n - 1] = st_ref[1:n]
    so_ref[n - 1] = un


def _conv_sample(state, u_new, conv_w, conv_b, ln_g, ln_b, bb):
    n, nb, _ = state.shape
    assert n == CONV_TAPS - 1 and nb % bb == 0
    vec = _resident((1, CONV_CH))
    return pl.pallas_call(
        _sample_conv_kernel,
        grid=(nb // bb,),
        in_specs=[pl.BlockSpec((n, bb, CONV_CH), lambda i: (0, i, 0)),
                  pl.BlockSpec((bb, CONV_CH), lambda i: (i, 0)),
                  _resident((CONV_TAPS, CONV_CH)), vec, vec, vec],
        out_specs=[pl.BlockSpec((bb, CONV_CH), lambda i: (i, 0)),
                   pl.BlockSpec((n, bb, CONV_CH), lambda i: (0, i, 0))],
        out_shape=[jax.ShapeDtypeStruct((nb, CONV_CH), BF), jax.ShapeDtypeStruct(state.shape, F32)],
        compiler_params=_params(1),
        name="conv_sample",
    )(state, u_new, conv_w, conv_b, ln_g, ln_b)


def _sample_attn_kernel(qt_ref, knt_ref, vnt_ref, k1, v1, k2, v2, k3, v3, o_ref, ok1, ov1, ok2, ov2, ok3, ov3):
    kin, vin = (k1, k2, k3), (v1, v2, v3)
    kout, vout = (ok1, ok2, ok3), (ov1, ov2, ov3)
    o_tile = jnp.zeros((HEAD_DIM, HEAD_CHUNK), F32)
    o_lane = lax.broadcasted_iota(jnp.int32, (HEAD_DIM, HEAD_CHUNK), 1)
    for hh in range(HEAD_CHUNK):
        rows = slice(hh * HEAD_DIM, (hh + 1) * HEAD_DIM)
        scores, new_scores, m = [], [], None
        for g in range(N_GROUPS):
            lb = kin[g].shape[1]
            cl = g * HEAD_CHUNK + hh
            qc = qt_ref[:, cl:cl + 1]
            s = jnp.sum(kin[g][rows, :] * qc, axis=0, keepdims=True)
            if DILATIONS[g] > 1:
                pos = lax.broadcasted_iota(jnp.int32, (1, lb), 1)
                s = jnp.where((pos & (DILATIONS[g] - 1)) == 0, s, NEG)
            sn = jnp.sum(knt_ref[:, cl:cl + 1] * qc, axis=0, keepdims=True)
            mg = jnp.maximum(jnp.max(s, axis=1, keepdims=True), sn)
            m = mg if m is None else jnp.maximum(m, mg)
            scores.append(s)
            new_scores.append(sn)
        l = jnp.zeros((1, 1), F32)
        acc = jnp.zeros((HEAD_DIM, 1), F32)
        for g in range(N_GROUPS):
            cl = g * HEAD_CHUNK + hh
            p = jnp.exp2(scores[g] - m)
            pn = jnp.exp2(new_scores[g] - m)
            l = l + jnp.sum(p, axis=1, keepdims=True) + pn
            acc = acc + jnp.sum(vin[g][rows, :] * p, axis=1, keepdims=True) + vnt_ref[:, cl:cl + 1] * pn
        o_tile = jnp.where(o_lane == hh, acc / l, o_tile)
        for g in range(N_GROUPS):
            lb = kin[g].shape[1]
            cl = g * HEAD_CHUNK + hh
            last = lax.broadcasted_iota(jnp.int32, (HEAD_DIM, lb), 1) == lb - 1
            kout[g][rows, :] = jnp.where(last, knt_ref[:, cl:cl + 1], pltpu.roll(kin[g][rows, :], lb - 1, 1))
            vout[g][rows, :] = jnp.where(last, vnt_ref[:, cl:cl + 1], pltpu.roll(vin[g][rows, :], lb - 1, 1))
    o_ref[...] = o_tile


def _columns(t):
    n = t.shape[0]
    t = t.reshape(n, N_GROUPS, N_HEADS // HEAD_CHUNK, HEAD_CHUNK, HEAD_DIM)
    return t.transpose(0, 2, 4, 1, 3).reshape(n, N_HEADS // HEAD_CHUNK, HEAD_DIM, N_GROUPS * HEAD_CHUNK)


def _attention_sample(q, k_new, v_new, caches):
    nb = q.shape[0]
    nch = N_HEADS // HEAD_CHUNK
    rows = HEAD_CHUNK * HEAD_DIM
    for g, c in enumerate(caches):
        assert c.shape == (nb, ATTN_W, WINDOWS[g // 2]), "window buffers are expected full"
    col_spec = pl.BlockSpec((None, None, HEAD_DIM, N_GROUPS * HEAD_CHUNK), lambda b, h: (b, h, 0, 0))
    buf_specs = [pl.BlockSpec((None, rows, c.shape[2]), lambda b, h: (b, h, 0)) for c in caches]
    outs = pl.pallas_call(
        _sample_attn_kernel,
        grid=(nb, nch),
        in_specs=[col_spec] * 3 + buf_specs,
        out_specs=[pl.BlockSpec((None, None, HEAD_DIM, HEAD_CHUNK), lambda b, h: (b, h, 0, 0))] + buf_specs,
        out_shape=[jax.ShapeDtypeStruct((nb, nch, HEAD_DIM, HEAD_CHUNK), F32)]
                  + [jax.ShapeDtypeStruct(c.shape, F32) for c in caches],
        compiler_params=_params(2),
        name="sample_attn",
    )(_columns(q), _columns(k_new), _columns(v_new), *caches)
    o = outs[0].transpose(0, 1, 3, 2).reshape(nb, ATTN_W)
    return o, outs[1:]


def _main_kernel(combine, x_ref, *refs):
    if combine:
        o_refs, l_refs, e_ref = refs[0:3], refs[3:6], refs[6]
        refs = refs[7:]
        ca_ref, sg_ref, wa_ref, wc_ref, wo_ref, g2_ref, w1_ref, w2_ref, y_ref, hid_ref, il_ref = refs
    else:
        o_ref = refs[0]
        ca_ref, sg_ref, wa_ref, wc_ref, wo_ref, g2_ref, w1_ref, w2_ref, y_ref, hid_ref = refs[1:]
    tm = x_ref.shape[0]
    if combine:
        slot = [0]

        def token_major(ref, g):
            d = DILATIONS[g]
            if d == 1:
                return ref[...].astype(F32)
            nch = ref.shape[-1] // LANES
            base = slot[0]
            slot[0] += nch
            for r in range(d):
                t = ref[r].astype(F32)
                for c in range(nch):
                    il_ref[base + c, pl.ds(r, tm // d, stride=d), :] = t[:, c * LANES:(c + 1) * LANES]
            return jnp.concatenate([il_ref[base + c] for c in range(nch)], axis=-1)

        ls = [token_major(l_refs[g], g) for g in range(N_GROUPS)]
        m = jnp.maximum(jnp.maximum(ls[0], ls[1]), ls[2])
        es = [jnp.exp(t - m) for t in ls]
        inv = 1.0 / (es[0] + es[1] + es[2])
        o = None
        for g in range(N_GROUPS):
            alpha = es[g] * inv
            hi = alpha.astype(BF)
            lo = (alpha - hi.astype(F32)).astype(BF)
            wide = (jnp.dot(hi, e_ref[...], preferred_element_type=F32)
                    + jnp.dot(lo, e_ref[...], preferred_element_type=F32))
            t = wide * token_major(o_refs[g], g)
            o = t if o is None else o + t
    else:
        o = o_ref[...]
    a = jnp.dot(o.astype(BF), wa_ref[...], preferred_element_type=F32)
    c = jnp.dot(ca_ref[...], wc_ref[...], preferred_element_type=F32)
    merged = sg_ref[:, :D_MODEL].astype(F32) * a + sg_ref[:, D_MODEL:].astype(F32) * c
    h = x_ref[...] + jnp.dot(merged.astype(BF), wo_ref[...], preferred_element_type=F32)
    hn = (h * lax.rsqrt(jnp.mean(h * h, axis=-1, keepdims=True) + EPS) * g2_ref[...]).astype(BF)
    fch = 512
    for c0 in range(0, D_FF, fch):
        t = jnp.maximum(jnp.dot(hn, w1_ref[:, c0:c0 + fch], preferred_element_type=F32), 0.0)
        hid_ref[:, c0:c0 + fch] = (t * t).astype(BF)
    y_ref[...] = h + jnp.dot(hid_ref[...], w2_ref[...], preferred_element_type=F32)


def _main(x2d, attn, ca, sg, weights, tm, nt=1):
    M = x2d.shape[0]
    assert M % (tm * nt) == 0
    B = M // (tm * nt)
    row = lambda width: pl.BlockSpec((tm, width), lambda b, i: (b * nt + i, 0))
    combine = isinstance(attn, tuple)
    scratch = [pltpu.VMEM((tm, D_FF), BF)]
    if combine:
        os_, ls_ = attn

        def group_spec(d, width):
            if d == 1:
                return row(width)
            return pl.BlockSpec((None, d, tm // d, width), lambda b, i: (b, 0, i, 0))

        head = jnp.arange(ATTN_W) // HEAD_DIM
        spread = (jnp.arange(LANES)[:, None] == head[None, :]).astype(BF)
        attn_args = list(os_) + list(ls_) + [spread]
        attn_specs = ([group_spec(d, ATTN_W) for d in DILATIONS] + [group_spec(d, LANES) for d in DILATIONS]
                      + [_resident((LANES, ATTN_W))])
        n_il = sum((ATTN_W + LANES) // LANES for d in DILATIONS if d > 1)
        scratch.append(pltpu.VMEM((n_il, tm, LANES), F32))
    else:
        attn_args, attn_specs = [attn], [row(ATTN_W)]
    return pl.pallas_call(
        functools.partial(_main_kernel, combine),
        grid=(B, nt),
        in_specs=[row(D_MODEL)] + attn_specs + [row(CONV_CH), row(2 * D_MODEL)]
                 + [_resident(w.shape) for w in weights],
        out_specs=row(D_MODEL),
        out_shape=jax.ShapeDtypeStruct((M, D_MODEL), F32),
        scratch_shapes=scratch,
        compiler_params=_params(2),
        name="main_prompt" if combine else "main_sample",
    )(x2d, *attn_args, ca, sg, *weights)


def kernel(x_prompt, x_sample, cache_k1, cache_v1, cache_k2, cache_v2, cache_k3, cache_v3, state_conv,
           norm1_g, w_in, q_norm_g, k_norm_g, conv_w, conv_b, conv_ln_g, conv_ln_b,
           w_attn_out, w_conv_out, w_o, norm2_g, w_ff1, w_ff2):
    assert w_in.shape[0] == 1, "one layer"
    B, S, _ = x_prompt.shape
    nb, ns, _ = x_sample.shape
    assert ns == 1, "one new token per sample row"
    tm = 256
    consts = _inproj_consts(norm1_g[0], w_in[0], q_norm_g[0], k_norm_g[0])
    conv_args = (conv_w[0], conv_b, conv_ln_g, conv_ln_b)
    weights = (w_attn_out[0].astype(BF), w_conv_out[0].astype(BF), w_o[0].astype(BF), norm2_g,
               w_ff1[0].astype(BF), w_ff2[0].astype(BF))

    outs = _inproj_prompt(x_prompt, consts, tm)
    qs, ks, vs = outs[0:3], outs[3:6], outs[6:9]
    u, sg = outs[9:11]
    kts, vts, ut = outs[11:14], outs[14:17], outs[17]
    attn_o, attn_l = [], []
    for g, d in enumerate(DILATIONS):
        as4 = lambda t: t.reshape(B, d, S // d, ATTN_W)
        o, lse = _attention_prompt(as4(qs[g]), as4(ks[g]), as4(vs[g]), g)
        attn_o.append(o.reshape(B * S, ATTN_W) if d == 1 else o)
        attn_l.append(lse.reshape(B * S, LANES) if d == 1 else lse)
    ca = _conv_prompt(u, B, S, *conv_args, tc=256)
    y_prompt = _main(x_prompt.reshape(B * S, D_MODEL), (tuple(attn_o), tuple(attn_l)), ca, sg, weights,
                     tm=tm, nt=S // tm).reshape(B, S, D_MODEL)
    tail = lambda t: t.reshape(1, B, t.shape[1], N_HEADS, HEAD_DIM)
    p_conv = ut[None, :, HALO - (CONV_TAPS - 1):, :]

    q_s, k_s, v_s, u_s, sg_s = _inproj_sample(x_sample.reshape(nb, D_MODEL), consts)
    to_buf = lambda c: c[0].transpose(0, 2, 3, 1).reshape(nb, ATTN_W, c.shape[2])
    from_buf = lambda c: c.reshape(nb, N_HEADS, HEAD_DIM, c.shape[2]).transpose(0, 3, 1, 2)[None]
    o_s, bufs = _attention_sample(q_s, k_s, v_s, [to_buf(c) for c in
                                                  (cache_k1, cache_v1, cache_k2, cache_v2, cache_k3, cache_v3)])
    ca_s, state_new = _conv_sample(state_conv[0].transpose(1, 0, 2), u_s, *conv_args, bb=min(32, nb))
    y_sample = _main(x_sample.reshape(nb, D_MODEL), o_s, ca_s, sg_s, weights, tm=nb).reshape(nb, 1, D_MODEL)
    s_conv = state_new.transpose(1, 0, 2)[None]

    kv_tails = [tail(t) for pair in zip(kts, vts) for t in pair]
    return (y_prompt, y_sample, *kv_tails, p_conv, *(from_buf(b) for b in bufs), s_conv)
```

```python
import functools
import math

import jax
import jax.numpy as jnp
from jax import lax
from jax.experimental import pallas as pl
from jax.experimental.pallas import tpu as pltpu

F32, BF = jnp.float32, jnp.bfloat16

D_MODEL = 1024
N_GROUPS = 3
WINDOWS = (128, 512, 2048)
DILATIONS = (1, 4, 16)
N_HEADS = 8
HEAD_DIM = 64
ATTN_W = N_HEADS * HEAD_DIM
QKV_W = N_GROUPS * ATTN_W
CONV_CH = D_MODEL
CONV_TAPS = 31
D_FF = 4 * D_MODEL
IN_W = 3 * QKV_W + 2 * CONV_CH + 2 * D_MODEL
EPS = 1e-6
LOG2E = math.log2(math.e)
LN2 = math.log(2.0)
Q_SCALE = LOG2E / 8.0
BAND = 128
NEG = -1e30

LANES = 128
SUBLANES = 8
PAIR = 2 * HEAD_DIM
HALO = 32
PCH = 512
NORM_W = 256
HEAD_CHUNK = 4
VMEM_LIMIT = 56 << 20


def _params(n_axes, vmem=VMEM_LIMIT):
    return pltpu.CompilerParams(dimension_semantics=("arbitrary",) * n_axes, vmem_limit_bytes=vmem)


def _resident(shape):
    nd = len(shape)
    return pl.BlockSpec(shape, lambda *_: (0,) * nd, pipeline_mode=pl.Buffered(1))


def _inproj_kernel(prompt, x_ref, g1_ref, w_ref, gq_ref, gk_ref, p_ref, *refs):
    if prompt:
        qs, ks, vs = refs[0:3], refs[3:6], refs[6:9]
        u_ref, sg_ref = refs[9:11]
        kts, vts = refs[11:14], refs[14:17]
        ut_ref, stage_ref = refs[17:19]
    else:
        q_ref, k_ref, v_ref, u_ref, sg_ref = refs
    tm = x_ref.shape[0]
    x = x_ref[...]
    xn = (x * lax.rsqrt(jnp.mean(x * x, axis=-1, keepdims=True) + EPS) * g1_ref[...]).astype(BF)

    def proj(c0):
        return jnp.dot(xn, w_ref[:, c0:c0 + PCH], preferred_element_type=F32)

    def head_norm(t, gain):
        parts = []
        for c0 in range(0, PCH, NORM_W):
            th = t[:, c0:c0 + NORM_W]
            ms = jnp.dot((th * th).astype(BF), p_ref[...], preferred_element_type=F32)
            parts.append(th * lax.rsqrt(ms + EPS) * gain[:, c0:c0 + NORM_W])
        return jnp.concatenate(parts, axis=-1)

    slot = [0]

    def put(dst, g, off, t):
        d = DILATIONS[g]
        if d == 1:
            dst[:, off:off + PCH] = t.astype(dst.dtype)
            return
        base = slot[0]
        slot[0] += PCH // LANES
        for c in range(PCH // LANES):
            stage_ref[base + c] = t[:, c * LANES:(c + 1) * LANES]
        for r in range(d):
            parts = [stage_ref[base + c, pl.ds(r, tm // d, stride=d), :] for c in range(PCH // LANES)]
            dst[r, :, off:off + PCH] = jnp.concatenate(parts, axis=-1).astype(dst.dtype)

    for c in range(QKV_W // PCH):
        cs = slice(c * PCH, (c + 1) * PCH)
        g, off = divmod(c * PCH, ATTN_W)
        qn = head_norm(proj(c * PCH), gq_ref[:, cs])
        kn = head_norm(proj(QKV_W + c * PCH), gk_ref[:, cs])
        vv = proj(2 * QKV_W + c * PCH)
        if prompt:
            put(qs[g], g, off, qn)
            put(ks[g], g, off, kn)
            put(vs[g], g, off, vv)
            rows = kts[g].shape[0]
            kts[g][:, off:off + PCH] = kn[tm - rows:, :]
            vts[g][:, off:off + PCH] = vv[tm - rows:, :]
        else:
            q_ref[:, cs] = qn
            k_ref[:, cs] = kn
            v_ref[:, cs] = vv
    for c in range(CONV_CH // PCH):
        cs = slice(c * PCH, (c + 1) * PCH)
        u = proj(3 * QKV_W + c * PCH) * jax.nn.sigmoid(proj(3 * QKV_W + CONV_CH + c * PCH))
        u_ref[:, cs] = u.astype(u_ref.dtype)
        if prompt:
            ut_ref[:, cs] = u[tm - HALO:, :]
    for c in range(2 * D_MODEL // PCH):
        cs = slice(c * PCH, (c + 1) * PCH)
        sg_ref[:, cs] = jax.nn.sigmoid(proj(3 * QKV_W + 2 * CONV_CH + c * PCH)).astype(sg_ref.dtype)


def _inproj_consts(norm1_g, w_in, q_norm_g, k_norm_g):
    gq = (jnp.repeat(q_norm_g, N_HEADS, axis=0) * Q_SCALE).reshape(1, QKV_W)
    gk = jnp.repeat(k_norm_g, N_HEADS, axis=0).reshape(1, QKV_W)
    head = jnp.arange(NORM_W) // HEAD_DIM
    pmat = jnp.where(head[:, None] == head[None, :], 1.0 / HEAD_DIM, 0.0).astype(BF)
    return norm1_g.reshape(1, D_MODEL), w_in.astype(BF), gq, gk, pmat


def _const_specs():
    return [_resident((1, D_MODEL)), _resident((D_MODEL, IN_W)), _resident((1, QKV_W)),
            _resident((1, QKV_W)), _resident((NORM_W, NORM_W))]


def _inproj_prompt(x, consts, tm):
    B, S, _ = x.shape
    nt = S // tm
    assert S % tm == 0 and S >= max(WINDOWS) and tm >= HALO
    assert all(tm % (2 * SUBLANES * d) == 0 for d in DILATIONS), "a residue's rows must fill bf16 tiles"
    row = lambda width: pl.BlockSpec((tm, width), lambda b, i: (b * nt + i, 0))

    def group_spec(d):
        if d == 1:
            return row(ATTN_W)
        return pl.BlockSpec((None, d, tm // d, ATTN_W), lambda b, i: (b, 0, i, 0))

    def group_shape(d):
        return jax.ShapeDtypeStruct((B * S, ATTN_W) if d == 1 else (B, d, S // d, ATTN_W), BF)

    def tail_spec(w):
        if w >= tm:
            assert w % tm == 0
            first = (S - w) // tm
            return pl.BlockSpec((None, tm, ATTN_W), lambda b, i: (b, jnp.maximum(i - first, 0), 0))
        assert tm % w == 0
        return pl.BlockSpec((None, w, ATTN_W), lambda b, i: (b, 0, 0))

    groups = [group_spec(d) for d in DILATIONS]
    group_shapes = [group_shape(d) for d in DILATIONS]
    tails = [tail_spec(w) for w in WINDOWS]
    tail_shapes = [jax.ShapeDtypeStruct((B, w, ATTN_W), F32) for w in WINDOWS]
    n_stage = 3 * sum(ATTN_W // LANES for d in DILATIONS if d > 1)
    M = B * S
    return pl.pallas_call(
        functools.partial(_inproj_kernel, True),
        grid=(B, nt),
        in_specs=[row(D_MODEL)] + _const_specs(),
        out_specs=groups * 3 + [row(CONV_CH), row(2 * D_MODEL)] + tails + tails
                  + [pl.BlockSpec((None, HALO, CONV_CH), lambda b, i: (b, 0, 0))],
        out_shape=group_shapes * 3
                  + [jax.ShapeDtypeStruct((M, CONV_CH), BF), jax.ShapeDtypeStruct((M, 2 * D_MODEL), BF)]
                  + tail_shapes + tail_shapes + [jax.ShapeDtypeStruct((B, HALO, CONV_CH), F32)],
        scratch_shapes=[pltpu.VMEM((n_stage, tm, LANES), F32)],
        compiler_params=_params(2),
        name="inproj_prompt",
    )(x.reshape(M, D_MODEL), *consts)


def _inproj_sample(x2d, consts):
    M = x2d.shape[0]
    full = lambda width: pl.BlockSpec((M, width), lambda i: (0, 0))
    return pl.pallas_call(
        functools.partial(_inproj_kernel, False),
        grid=(1,),
        in_specs=[full(D_MODEL)] + _const_specs(),
        out_specs=[full(QKV_W), full(QKV_W), full(QKV_W), full(CONV_CH), full(2 * D_MODEL)],
        out_shape=[jax.ShapeDtypeStruct((M, QKV_W), F32)] * 3
                  + [jax.ShapeDtypeStruct((M, CONV_CH), F32), jax.ShapeDtypeStruct((M, 2 * D_MODEL), BF)],
        compiler_params=_params(1),
        name="inproj_sample",
    )(x2d, *consts)


def _attn_kernel(q_ref, kc_ref, kp_ref, vc_ref, vp_ref, o_ref, lse_ref, kx_ref, vx_ref):
    tq = q_ref.shape[0]
    not_first = pl.program_id(2) > 0
    kt = jnp.concatenate([kp_ref[...].T, kc_ref[...].T], axis=1)
    zero = jnp.zeros((HEAD_DIM, tq + BAND), BF)
    for h in range(N_HEADS):
        kh = kt[h * HEAD_DIM:(h + 1) * HEAD_DIM]
        lo, hi = (kh, zero) if h % 2 == 0 else (zero, kh)
        kx_ref[h * PAIR:h * PAIR + HEAD_DIM, :] = lo
        kx_ref[h * PAIR + HEAD_DIM:(h + 1) * PAIR, :] = hi
    ones = jnp.ones((tq + BAND, PAIR), BF)
    for pr in range(N_HEADS // 2):
        vx_ref[0:BAND, 2 * pr * PAIR:(2 * pr + 1) * PAIR] = vp_ref[:, pr * PAIR:(pr + 1) * PAIR]
        vx_ref[BAND:, 2 * pr * PAIR:(2 * pr + 1) * PAIR] = vc_ref[:, pr * PAIR:(pr + 1) * PAIR]
        vx_ref[:, (2 * pr + 1) * PAIR:(2 * pr + 2) * PAIR] = ones
    a = lax.broadcasted_iota(jnp.int32, (BAND, 2 * BAND), 0)
    c = lax.broadcasted_iota(jnp.int32, (BAND, 2 * BAND), 1)
    band = jnp.logical_and(c >= a, c <= a + BAND)
    band0 = jnp.logical_and(band, jnp.logical_or(c >= BAND, not_first))
    lane = lax.broadcasted_iota(jnp.int32, (BAND, LANES), 1)
    for sb in range(tq // BAND):
        rows = slice(sb * BAND, (sb + 1) * BAND)
        keys = slice(sb * BAND, (sb + 2) * BAND)
        mask = band0 if sb == 0 else band
        q = q_ref[rows, :]
        m_t = jnp.zeros((BAND, LANES), F32)
        l_t = jnp.ones((BAND, LANES), F32)
        for pr in range(N_HEADS // 2):
            vv = vx_ref[keys, 2 * pr * PAIR:(2 * pr + 2) * PAIR]
            pair = None
            for hh in range(2):
                h = 2 * pr + hh
                s = jnp.dot(q[:, pr * PAIR:(pr + 1) * PAIR], kx_ref[h * PAIR:(h + 1) * PAIR, keys],
                            preferred_element_type=F32)
                s = jnp.where(mask, s, NEG)
                m = jnp.max(s, axis=-1, keepdims=True)
                p = jnp.exp2(s - m)
                r = jnp.dot(p.astype(BF), vv, preferred_element_type=F32)
                l = r[:, PAIR:]
                contrib = r[:, :PAIR] / l
                pair = contrib if hh == 0 else jnp.where(lane < HEAD_DIM, pair, contrib)
                m_t = jnp.where(lane == h, m, m_t)
                l_t = jnp.where(lane == h, l, l_t)
            o_ref[rows, pr * PAIR:(pr + 1) * PAIR] = pair.astype(o_ref.dtype)
        lse_ref[rows, :] = m_t * LN2 + jnp.log(l_t)


def _attention_prompt(q, k, v, g):
    B, d, L, _ = q.shape
    assert WINDOWS[g] // d == BAND and L % BAND == 0
    tq = min(512, L)
    assert L % tq == 0
    sub = tq // BAND
    cur = pl.BlockSpec((None, None, tq, ATTN_W), lambda b, r, j: (b, r, j, 0))
    prev = pl.BlockSpec((None, None, BAND, ATTN_W), lambda b, r, j: (b, r, jnp.maximum(j * sub - 1, 0), 0))
    return pl.pallas_call(
        _attn_kernel,
        grid=(B, d, L // tq),
        in_specs=[cur, cur, prev, cur, prev],
        out_specs=[cur, pl.BlockSpec((None, None, tq, LANES), lambda b, r, j: (b, r, j, 0))],
        out_shape=[jax.ShapeDtypeStruct((B, d, L, ATTN_W), BF), jax.ShapeDtypeStruct((B, d, L, LANES), F32)],
        scratch_shapes=[pltpu.VMEM((N_HEADS * PAIR, tq + BAND), BF), pltpu.VMEM((tq + BAND, 2 * ATTN_W), BF)],
        compiler_params=_params(3),
        name="attn_g%d" % (g + 1),
    )(q, k, k, v, v)


def _ln_swish(c, g, b):
    mu = jnp.mean(c, axis=-1, keepdims=True)
    cc = c - mu
    var = jnp.mean(cc * cc, axis=-1, keepdims=True)
    y = cc * lax.rsqrt(var + EPS) * g + b
    return y * jax.nn.sigmoid(y)


def _conv_kernel(u_ref, halo_ref, w_ref, b_ref, lg_ref, lb_ref, o_ref, ext_ref, cv_ref):
    tc = u_ref.shape[0]
    rc = 64
    n = tc + HALO
    ext_ref[0, 0:HALO, :] = jnp.where(pl.program_id(1) > 0, halo_ref[...].astype(F32), 0.0)
    ext_ref[0, HALO:n, :] = u_ref[...].astype(F32)
    for s in range(1, SUBLANES):
        ext_ref[s, 0:n - SUBLANES, :] = ext_ref[0, s:n - SUBLANES + s, :]
    lead = HALO - (CONV_TAPS - 1)

    def row_chunk(i, carry):
        r0 = pl.multiple_of(i * rc, rc)
        for l0 in range(0, CONV_CH, LANES):
            ls = slice(l0, l0 + LANES)
            acc = jnp.broadcast_to(b_ref[:, ls], (rc, LANES))
            taps = w_ref[:, ls]
            for j in range(CONV_TAPS):
                a8, s = divmod(lead + j, SUBLANES)
                acc = acc + taps[j:j + 1, :] * ext_ref[s, pl.ds(r0 + a8 * SUBLANES, rc), ls]
            cv_ref[pl.ds(r0, rc), ls] = acc
        o_ref[pl.ds(r0, rc), :] = _ln_swish(cv_ref[pl.ds(r0, rc), :], lg_ref[...], lb_ref[...]).astype(o_ref.dtype)
        return carry

    lax.fori_loop(0, tc // rc, row_chunk, 0)


def _conv_prompt(u, B, S, conv_w, conv_b, ln_g, ln_b, tc):
    nt = S // tc
    assert S % tc == 0 and tc % HALO == 0
    hb = tc // HALO
    vec = _resident((1, CONV_CH))
    return pl.pallas_call(
        _conv_kernel,
        grid=(B, nt),
        in_specs=[pl.BlockSpec((tc, CONV_CH), lambda b, i: (b * nt + i, 0)),
                  pl.BlockSpec((HALO, CONV_CH), lambda b, i: (jnp.maximum((b * nt + i) * hb - 1, 0), 0)),
                  _resident((CONV_TAPS, CONV_CH)), vec, vec, vec],
        out_specs=pl.BlockSpec((tc, CONV_CH), lambda b, i: (b * nt + i, 0)),
        out_shape=jax.ShapeDtypeStruct((B * S, CONV_CH), BF),
        scratch_shapes=[pltpu.VMEM((SUBLANES, tc + HALO, CONV_CH), F32), pltpu.VMEM((tc, CONV_CH), F32)],
        compiler_params=_params(2),
        name="conv_prompt",
    )(u, u, conv_w, conv_b, ln_g, ln_b)


def _sample_conv_kernel(st_ref, un_ref, w_ref, b_ref, lg_ref, lb_ref, ca_ref, so_ref):
    n = CONV_TAPS - 1
    un = un_ref[...]
    acc = b_ref[...] + w_ref[n:n + 1, :] * un
    for j in range(n):
        acc = acc + w_ref[j:j + 1, :] * st_ref[j]
    ca_ref[...] = _ln_swish(acc, lg_ref[...], lb_ref[...]).astype(ca_ref.dtype)
    so_ref[0:n - 1] = st_ref[1:n]
    so_ref[n - 1] = un


def _conv_sample(state, u_new, conv_w, conv_b, ln_g, ln_b, bb):
    n, nb, _ = state.shape
    assert n == CONV_TAPS - 1 and nb % bb == 0
    vec = _resident((1, CONV_CH))
    return pl.pallas_call(
        _sample_conv_kernel,
        grid=(nb // bb,),
        in_specs=[pl.BlockSpec((n, bb, CONV_CH), lambda i: (0, i, 0)),
                  pl.BlockSpec((bb, CONV_CH), lambda i: (i, 0)),
                  _resident((CONV_TAPS, CONV_CH)), vec, vec, vec],
        out_specs=[pl.BlockSpec((bb, CONV_CH), lambda i: (i, 0)),
                   pl.BlockSpec((n, bb, CONV_CH), lambda i: (0, i, 0))],
        out_shape=[jax.ShapeDtypeStruct((nb, CONV_CH), BF), jax.ShapeDtypeStruct(state.shape, F32)],
        compiler_params=_params(1),
        name="conv_sample",
    )(state, u_new, conv_w, conv_b, ln_g, ln_b)


def _sample_attn_kernel(qt_ref, knt_ref, vnt_ref, k1, v1, k2, v2, k3, v3,
                        o_ref, ok1, ov1, ok2, ov2, ok3, ov3, bc_ref):
    for i, src in enumerate((qt_ref, knt_ref, vnt_ref)):
        for cl in range(N_GROUPS * HEAD_CHUNK):
            bc_ref[i, :, cl * LANES:(cl + 1) * LANES] = jnp.broadcast_to(src[:, cl:cl + 1], (HEAD_DIM, LANES))
    col = lambda i, cl: bc_ref[i, :, cl * LANES:(cl + 1) * LANES]
    kin, vin = (k1, k2, k3), (v1, v2, v3)
    kout, vout = (ok1, ok2, ok3), (ov1, ov2, ov3)
    head_rows = [slice(hh * HEAD_DIM, (hh + 1) * HEAD_DIM) for hh in range(HEAD_CHUNK)]
    scores, new_scores, m = [], [], None
    for g in range(N_GROUPS):
        lb = kin[g].shape[1]
        rows_s, rows_n = [], []
        for hh in range(HEAD_CHUNK):
            cl = g * HEAD_CHUNK + hh
            qc = col(0, cl)
            rows_s.append(jnp.sum(kin[g][head_rows[hh], :] * pltpu.repeat(qc, lb // LANES, axis=1),
                                  axis=0, keepdims=True))
            rows_n.append(jnp.sum(col(1, cl) * qc, axis=0, keepdims=True)[:, 0:1])
        s = jnp.concatenate(rows_s, axis=0)
        sn = jnp.concatenate(rows_n, axis=0)
        if DILATIONS[g] > 1:
            pos = lax.broadcasted_iota(jnp.int32, (HEAD_CHUNK, lb), 1)
            s = jnp.where((pos & (DILATIONS[g] - 1)) == 0, s, NEG)
        mg = jnp.maximum(jnp.max(s, axis=1, keepdims=True), sn)
        m = mg if m is None else jnp.maximum(m, mg)
        scores.append(s)
        new_scores.append(sn)
    probs = [jnp.exp2(s - m) for s in scores]
    new_probs = [jnp.exp2(sn - m) for sn in new_scores]
    l = None
    for g in range(N_GROUPS):
        t = jnp.sum(probs[g], axis=1, keepdims=True) + new_probs[g]
        l = t if l is None else l + t
    inv = 1.0 / l
    o_tile = jnp.zeros((HEAD_DIM, HEAD_CHUNK), F32)
    o_lane = lax.broadcasted_iota(jnp.int32, (HEAD_DIM, HEAD_CHUNK), 1)
    for hh in range(HEAD_CHUNK):
        acc = None
        for g in range(N_GROUPS):
            cl = g * HEAD_CHUNK + hh
            t = (jnp.sum(vin[g][head_rows[hh], :] * probs[g][hh:hh + 1, :], axis=1, keepdims=True)
                 + col(2, cl)[:, 0:1] * new_probs[g][hh:hh + 1, :])
            acc = t if acc is None else acc + t
        o_tile = jnp.where(o_lane == hh, acc * inv[hh:hh + 1, :], o_tile)
    o_ref[...] = o_tile
    last = lax.broadcasted_iota(jnp.int32, (HEAD_DIM, LANES), 1) == LANES - 1

    def shift_in(dst, src, rows, new_col):
        lb = src.shape[1]
        rolled = pltpu.roll(src[rows, :], lb - 1, 1)
        if lb > LANES:
            dst[rows, 0:lb - LANES] = rolled[:, 0:lb - LANES]
        dst[rows, lb - LANES:lb] = jnp.where(last, new_col, rolled[:, lb - LANES:lb])

    for g in range(N_GROUPS):
        for hh in range(HEAD_CHUNK):
            cl = g * HEAD_CHUNK + hh
            shift_in(kout[g], kin[g], head_rows[hh], col(1, cl))
            shift_in(vout[g], vin[g], head_rows[hh], col(2, cl))


def _columns(t):
    n = t.shape[0]
    t = t.reshape(n, N_GROUPS, N_HEADS // HEAD_CHUNK, HEAD_CHUNK, HEAD_DIM)
    return t.transpose(0, 2, 4, 1, 3).reshape(n, N_HEADS // HEAD_CHUNK, HEAD_DIM, N_GROUPS * HEAD_CHUNK)


def _attention_sample(q, k_new, v_new, caches):
    nb = q.shape[0]
    nch = N_HEADS // HEAD_CHUNK
    rows = HEAD_CHUNK * HEAD_DIM
    for g, c in enumerate(caches):
        assert c.shape == (nb, ATTN_W, WINDOWS[g // 2]), "window buffers are expected full"
    col_spec = pl.BlockSpec((None, None, HEAD_DIM, N_GROUPS * HEAD_CHUNK), lambda b, h: (b, h, 0, 0))
    buf_specs = [pl.BlockSpec((None, rows, c.shape[2]), lambda b, h: (b, h, 0)) for c in caches]
    outs = pl.pallas_call(
        _sample_attn_kernel,
        grid=(nb, nch),
        in_specs=[col_spec] * 3 + buf_specs,
        out_specs=[pl.BlockSpec((None, None, HEAD_DIM, HEAD_CHUNK), lambda b, h: (b, h, 0, 0))] + buf_specs,
        out_shape=[jax.ShapeDtypeStruct((nb, nch, HEAD_DIM, HEAD_CHUNK), F32)]
                  + [jax.ShapeDtypeStruct(c.shape, F32) for c in caches],
        scratch_shapes=[pltpu.VMEM((3, HEAD_DIM, N_GROUPS * HEAD_CHUNK * LANES), F32)],
        compiler_params=_params(2),
        name="sample_attn",
    )(_columns(q), _columns(k_new), _columns(v_new), *caches)
    o = outs[0].transpose(0, 1, 3, 2).reshape(nb, ATTN_W)
    return o, outs[1:]


def _main_kernel(combine, x_ref, *refs):
    if combine:
        o_refs, l_refs, e_ref = refs[0:3], refs[3:6], refs[6]
        refs = refs[7:]
        ca_ref, sg_ref, wa_ref, wc_ref, wo_ref, g2_ref, w1_ref, w2_ref, y_ref, hid_ref, il_ref = refs
    else:
        o_ref = refs[0]
        ca_ref, sg_ref, wa_ref, wc_ref, wo_ref, g2_ref, w1_ref, w2_ref, y_ref, hid_ref = refs[1:]
    tm = x_ref.shape[0]
    if combine:
        slot = [0]

        def token_major(ref, g):
            d = DILATIONS[g]
            if d == 1:
                return ref[...].astype(F32)
            nch = ref.shape[-1] // LANES
            base = slot[0]
            slot[0] += nch
            for r in range(d):
                t = ref[r].astype(F32)
                for c in range(nch):
                    il_ref[base + c, pl.ds(r, tm // d, stride=d), :] = t[:, c * LANES:(c + 1) * LANES]
            return jnp.concatenate([il_ref[base + c] for c in range(nch)], axis=-1)

        ls = [token_major(l_refs[g], g) for g in range(N_GROUPS)]
        m = jnp.maximum(jnp.maximum(ls[0], ls[1]), ls[2])
        es = [jnp.exp(t - m) for t in ls]
        inv = 1.0 / (es[0] + es[1] + es[2])
        o = None
        for g in range(N_GROUPS):
            alpha = es[g] * inv
            hi = alpha.astype(BF)
            lo = (alpha - hi.astype(F32)).astype(BF)
            wide = (jnp.dot(hi, e_ref[...], preferred_element_type=F32)
                    + jnp.dot(lo, e_ref[...], preferred_element_type=F32))
            t = wide * token_major(o_refs[g], g)
            o = t if o is None else o + t
    else:
        o = o_ref[...]
    a = jnp.dot(o.astype(BF), wa_ref[...], preferred_element_type=F32)
    c = jnp.dot(ca_ref[...], wc_ref[...], preferred_element_type=F32)
    merged = sg_ref[:, :D_MODEL].astype(F32) * a + sg_ref[:, D_MODEL:].astype(F32) * c
    h = x_ref[...] + jnp.dot(merged.astype(BF), wo_ref[...], preferred_element_type=F32)
    hn = (h * lax.rsqrt(jnp.mean(h * h, axis=-1, keepdims=True) + EPS) * g2_ref[...]).astype(BF)
    fch = 512
    for c0 in range(0, D_FF, fch):
        t = jnp.maximum(jnp.dot(hn, w1_ref[:, c0:c0 + fch], preferred_element_type=F32), 0.0)
        hid_ref[:, c0:c0 + fch] = (t * t).astype(BF)
    y_ref[...] = h + jnp.dot(hid_ref[...], w2_ref[...], preferred_element_type=F32)


def _main(x2d, attn, ca, sg, weights, tm, nt=1):
    M = x2d.shape[0]
    assert M % (tm * nt) == 0
    B = M // (tm * nt)
    row = lambda width: pl.BlockSpec((tm, width), lambda b, i: (b * nt + i, 0))
    combine = isinstance(attn, tuple)
    scratch = [pltpu.VMEM((tm, D_FF), BF)]
    if combine:
        os_, ls_ = attn

        def group_spec(d, width):
            if d == 1:
                return row(width)
            return pl.BlockSpec((None, d, tm // d, width), lambda b, i: (b, 0, i, 0))

        head = jnp.arange(ATTN_W) // HEAD_DIM
        spread = (jnp.arange(LANES)[:, None] == head[None, :]).astype(BF)
        attn_args = list(os_) + list(ls_) + [spread]
        attn_specs = ([group_spec(d, ATTN_W) for d in DILATIONS] + [group_spec(d, LANES) for d in DILATIONS]
                      + [_resident((LANES, ATTN_W))])
        n_il = sum((ATTN_W + LANES) // LANES for d in DILATIONS if d > 1)
        scratch.append(pltpu.VMEM((n_il, tm, LANES), F32))
    else:
        attn_args, attn_specs = [attn], [row(ATTN_W)]
    return pl.pallas_call(
        functools.partial(_main_kernel, combine),
        grid=(B, nt),
        in_specs=[row(D_MODEL)] + attn_specs + [row(CONV_CH), row(2 * D_MODEL)]
                 + [_resident(w.shape) for w in weights],
        out_specs=row(D_MODEL),
        out_shape=jax.ShapeDtypeStruct((M, D_MODEL), F32),
        scratch_shapes=scratch,
        compiler_params=_params(2),
        name="main_prompt" if combine else "main_sample",
    )(x2d, *attn_args, ca, sg, *weights)


def kernel(x_prompt, x_sample, cache_k1, cache_v1, cache_k2, cache_v2, cache_k3, cache_v3, state_conv,
           norm1_g, w_in, q_norm_g, k_norm_g, conv_w, conv_b, conv_ln_g, conv_ln_b,
           w_attn_out, w_conv_out, w_o, norm2_g, w_ff1, w_ff2):
    assert w_in.shape[0] == 1, "one layer"
    B, S, _ = x_prompt.shape
    nb, ns, _ = x_sample.shape
    assert ns == 1, "one new token per sample row"
    tm = 256
    consts = _inproj_consts(norm1_g[0], w_in[0], q_norm_g[0], k_norm_g[0])
    conv_args = (conv_w[0], conv_b, conv_ln_g, conv_ln_b)
    weights = (w_attn_out[0].astype(BF), w_conv_out[0].astype(BF), w_o[0].astype(BF), norm2_g,
               w_ff1[0].astype(BF), w_ff2[0].astype(BF))

    outs = _inproj_prompt(x_prompt, consts, tm)
    qs, ks, vs = outs[0:3], outs[3:6], outs[6:9]
    u, sg = outs[9:11]
    kts, vts, ut = outs[11:14], outs[14:17], outs[17]
    attn_o, attn_l = [], []
    for g, d in enumerate(DILATIONS):
        as4 = lambda t: t.reshape(B, d, S // d, ATTN_W)
        o, lse = _attention_prompt(as4(qs[g]), as4(ks[g]), as4(vs[g]), g)
        attn_o.append(o.reshape(B * S, ATTN_W) if d == 1 else o)
        attn_l.append(lse.reshape(B * S, LANES) if d == 1 else lse)
    ca = _conv_prompt(u, B, S, *conv_args, tc=256)
    y_prompt = _main(x_prompt.reshape(B * S, D_MODEL), (tuple(attn_o), tuple(attn_l)), ca, sg, weights,
                     tm=tm, nt=S // tm).reshape(B, S, D_MODEL)
    tail = lambda t: t.reshape(1, B, t.shape[1], N_HEADS, HEAD_DIM)
    p_conv = ut[None, :, HALO - (CONV_TAPS - 1):, :]

    q_s, k_s, v_s, u_s, sg_s = _inproj_sample(x_sample.reshape(nb, D_MODEL), consts)
    to_buf = lambda c: c[0].transpose(0, 2, 3, 1).reshape(nb, ATTN_W, c.shape[2])
    from_buf = lambda c: c.reshape(nb, N_HEADS, HEAD_DIM, c.shape[2]).transpose(0, 3, 1, 2)[None]
    o_s, bufs = _attention_sample(q_s, k_s, v_s, [to_buf(c) for c in
                                                  (cache_k1, cache_v1, cache_k2, cache_v2, cache_k3, cache_v3)])
    ca_s, state_new = _conv_sample(state_conv[0].transpose(1, 0, 2), u_s, *conv_args, bb=min(32, nb))
    y_sample = _main(x_sample.reshape(nb, D_MODEL), o_s, ca_s, sg_s, weights, tm=nb).reshape(nb, 1, D_MODEL)
    s_conv = state_new.transpose(1, 0, 2)[None]

    kv_tails = [tail(t) for pair in zip(kts, vts) for t in pair]
    return (y_prompt, y_sample, *kv_tails, p_conv, *(from_buf(b) for b in bufs), s_conv)
```

```python
import functools
import math

import jax
import jax.numpy as jnp
from jax import lax
from jax.experimental import pallas as pl
from jax.experimental.pallas import tpu as pltpu

F32, BF = jnp.float32, jnp.bfloat16

D_MODEL = 1024
N_GROUPS = 3
WINDOWS = (128, 512, 2048)
DILATIONS = (1, 4, 16)
N_HEADS = 8
HEAD_DIM = 64
ATTN_W = N_HEADS * HEAD_DIM
QKV_W = N_GROUPS * ATTN_W
CONV_CH = D_MODEL
CONV_TAPS = 31
D_FF = 4 * D_MODEL
IN_W = 3 * QKV_W + 2 * CONV_CH + 2 * D_MODEL
EPS = 1e-6
LOG2E = math.log2(math.e)
LN2 = math.log(2.0)
Q_SCALE = LOG2E / 8.0
BAND = 128
NEG = -1e30

LANES = 128
SUBLANES = 8
PAIR = 2 * HEAD_DIM
HALO = 32
PCH = 512
NORM_W = 256
HEAD_CHUNK = 4
VMEM_LIMIT = 60 << 20


def _params(n_axes, vmem=VMEM_LIMIT):
    return pltpu.CompilerParams(dimension_semantics=("arbitrary",) * n_axes, vmem_limit_bytes=vmem)


def _resident(shape):
    nd = len(shape)
    return pl.BlockSpec(shape, lambda *_: (0,) * nd, pipeline_mode=pl.Buffered(1))


def _inproj_kernel(prompt, x_ref, g1_ref, w_ref, gq_ref, gk_ref, p_ref, *refs):
    if prompt:
        qs, ks, vs = refs[0:3], refs[3:6], refs[6:9]
        u_ref, sg_ref = refs[9:11]
        kts, vts = refs[11:14], refs[14:17]
        ut_ref, stage_ref = refs[17:19]
    else:
        q_ref, k_ref, v_ref, u_ref, sg_ref = refs
    tm = x_ref.shape[0]
    x = x_ref[...]
    xn = (x * lax.rsqrt(jnp.mean(x * x, axis=-1, keepdims=True) + EPS) * g1_ref[...]).astype(BF)

    def proj(c0):
        return jnp.dot(xn, w_ref[:, c0:c0 + PCH], preferred_element_type=F32)

    def head_norm(t, gain):
        parts = []
        for c0 in range(0, PCH, NORM_W):
            th = t[:, c0:c0 + NORM_W]
            ms = jnp.dot((th * th).astype(BF), p_ref[...], preferred_element_type=F32)
            parts.append(th * lax.rsqrt(ms + EPS) * gain[:, c0:c0 + NORM_W])
        return jnp.concatenate(parts, axis=-1)

    slot = [0]

    def put(dst, g, off, t):
        d = DILATIONS[g]
        if d == 1:
            dst[:, off:off + PCH] = t.astype(dst.dtype)
            return
        base = slot[0] % stage_ref.shape[0]
        slot[0] += PCH // LANES
        for c in range(PCH // LANES):
            stage_ref[base + c] = t[:, c * LANES:(c + 1) * LANES]
        for r in range(d):
            parts = [stage_ref[base + c, pl.ds(r, tm // d, stride=d), :] for c in range(PCH // LANES)]
            dst[r, :, off:off + PCH] = jnp.concatenate(parts, axis=-1).astype(dst.dtype)

    for c in range(QKV_W // PCH):
        cs = slice(c * PCH, (c + 1) * PCH)
        g, off = divmod(c * PCH, ATTN_W)
        qn = head_norm(proj(c * PCH), gq_ref[:, cs])
        kn = head_norm(proj(QKV_W + c * PCH), gk_ref[:, cs])
        vv = proj(2 * QKV_W + c * PCH)
        if prompt:
            put(qs[g], g, off, qn)
            put(ks[g], g, off, kn)
            put(vs[g], g, off, vv)
            rows = kts[g].shape[0]
            kts[g][:, off:off + PCH] = kn[tm - rows:, :]
            vts[g][:, off:off + PCH] = vv[tm - rows:, :]
        else:
            q_ref[:, cs] = qn
            k_ref[:, cs] = kn
            v_ref[:, cs] = vv
    for c in range(CONV_CH // PCH):
        cs = slice(c * PCH, (c + 1) * PCH)
        u = proj(3 * QKV_W + c * PCH) * jax.nn.sigmoid(proj(3 * QKV_W + CONV_CH + c * PCH))
        u_ref[:, cs] = u.astype(u_ref.dtype)
        if prompt:
            ut_ref[:, cs] = u[tm - HALO:, :]
    for c in range(2 * D_MODEL // PCH):
        cs = slice(c * PCH, (c + 1) * PCH)
        sg_ref[:, cs] = jax.nn.sigmoid(proj(3 * QKV_W + 2 * CONV_CH + c * PCH)).astype(sg_ref.dtype)


def _inproj_consts(norm1_g, w_in, q_norm_g, k_norm_g):
    gq = (jnp.repeat(q_norm_g, N_HEADS, axis=0) * Q_SCALE).reshape(1, QKV_W)
    gk = jnp.repeat(k_norm_g, N_HEADS, axis=0).reshape(1, QKV_W)
    head = jnp.arange(NORM_W) // HEAD_DIM
    pmat = jnp.where(head[:, None] == head[None, :], 1.0 / HEAD_DIM, 0.0).astype(BF)
    return norm1_g.reshape(1, D_MODEL), w_in.astype(BF), gq, gk, pmat


def _const_specs():
    return [_resident((1, D_MODEL)), _resident((D_MODEL, IN_W)), _resident((1, QKV_W)),
            _resident((1, QKV_W)), _resident((NORM_W, NORM_W))]


def _inproj_prompt(x, consts, tm):
    B, S, _ = x.shape
    nt = S // tm
    assert S % tm == 0 and S >= max(WINDOWS) and tm >= HALO
    assert all(tm % (2 * SUBLANES * d) == 0 for d in DILATIONS), "a residue's rows must fill bf16 tiles"
    row = lambda width: pl.BlockSpec((tm, width), lambda b, i: (b * nt + i, 0))

    def group_spec(d):
        if d == 1:
            return row(ATTN_W)
        return pl.BlockSpec((None, d, tm // d, ATTN_W), lambda b, i: (b, 0, i, 0))

    def group_shape(d):
        return jax.ShapeDtypeStruct((B * S, ATTN_W) if d == 1 else (B, d, S // d, ATTN_W), BF)

    def tail_spec(w):
        if w >= tm:
            assert w % tm == 0
            first = (S - w) // tm
            return pl.BlockSpec((None, tm, ATTN_W), lambda b, i: (b, jnp.maximum(i - first, 0), 0))
        assert tm % w == 0
        return pl.BlockSpec((None, w, ATTN_W), lambda b, i: (b, 0, 0))

    groups = [group_spec(d) for d in DILATIONS]
    group_shapes = [group_shape(d) for d in DILATIONS]
    tails = [tail_spec(w) for w in WINDOWS]
    tail_shapes = [jax.ShapeDtypeStruct((B, w, ATTN_W), F32) for w in WINDOWS]
    n_stage = 2 * (PCH // LANES)
    M = B * S
    return pl.pallas_call(
        functools.partial(_inproj_kernel, True),
        grid=(B, nt),
        in_specs=[row(D_MODEL)] + _const_specs(),
        out_specs=groups * 3 + [row(CONV_CH), row(2 * D_MODEL)] + tails + tails
                  + [pl.BlockSpec((None, HALO, CONV_CH), lambda b, i: (b, 0, 0))],
        out_shape=group_shapes * 3
                  + [jax.ShapeDtypeStruct((M, CONV_CH), BF), jax.ShapeDtypeStruct((M, 2 * D_MODEL), BF)]
                  + tail_shapes + tail_shapes + [jax.ShapeDtypeStruct((B, HALO, CONV_CH), F32)],
        scratch_shapes=[pltpu.VMEM((n_stage, tm, LANES), F32)],
        compiler_params=_params(2),
        name="inproj_prompt",
    )(x.reshape(M, D_MODEL), *consts)


def _inproj_sample(x2d, consts):
    M = x2d.shape[0]
    full = lambda width: pl.BlockSpec((M, width), lambda i: (0, 0))
    return pl.pallas_call(
        functools.partial(_inproj_kernel, False),
        grid=(1,),
        in_specs=[full(D_MODEL)] + _const_specs(),
        out_specs=[full(QKV_W), full(QKV_W), full(QKV_W), full(CONV_CH), full(2 * D_MODEL)],
        out_shape=[jax.ShapeDtypeStruct((M, QKV_W), F32)] * 3
                  + [jax.ShapeDtypeStruct((M, CONV_CH), F32), jax.ShapeDtypeStruct((M, 2 * D_MODEL), BF)],
        compiler_params=_params(1),
        name="inproj_sample",
    )(x2d, *consts)


def _attn_kernel(q_ref, kc_ref, kp_ref, vc_ref, vp_ref, o_ref, lse_ref, kx_ref, vx_ref):
    for ri in range(q_ref.shape[0]):
        _attn_residue(q_ref.at[ri], kc_ref.at[ri], kp_ref.at[ri], vc_ref.at[ri], vp_ref.at[ri],
                      o_ref.at[ri], lse_ref.at[ri], kx_ref, vx_ref)


def _attn_residue(q_ref, kc_ref, kp_ref, vc_ref, vp_ref, o_ref, lse_ref, kx_ref, vx_ref):
    tq = q_ref.shape[0]
    not_first = pl.program_id(2) > 0
    kt = jnp.concatenate([kp_ref[...].T, kc_ref[...].T], axis=1)
    zero = jnp.zeros((HEAD_DIM, tq + BAND), BF)
    for h in range(N_HEADS):
        kh = kt[h * HEAD_DIM:(h + 1) * HEAD_DIM]
        lo, hi = (kh, zero) if h % 2 == 0 else (zero, kh)
        kx_ref[h * PAIR:h * PAIR + HEAD_DIM, :] = lo
        kx_ref[h * PAIR + HEAD_DIM:(h + 1) * PAIR, :] = hi
    ones = jnp.ones((tq + BAND, PAIR), BF)
    for pr in range(N_HEADS // 2):
        vx_ref[0:BAND, 2 * pr * PAIR:(2 * pr + 1) * PAIR] = vp_ref[:, pr * PAIR:(pr + 1) * PAIR]
        vx_ref[BAND:, 2 * pr * PAIR:(2 * pr + 1) * PAIR] = vc_ref[:, pr * PAIR:(pr + 1) * PAIR]
        vx_ref[:, (2 * pr + 1) * PAIR:(2 * pr + 2) * PAIR] = ones
    a = lax.broadcasted_iota(jnp.int32, (BAND, 2 * BAND), 0)
    c = lax.broadcasted_iota(jnp.int32, (BAND, 2 * BAND), 1)
    band = jnp.logical_and(c >= a, c <= a + BAND)
    band0 = jnp.logical_and(band, jnp.logical_or(c >= BAND, not_first))
    lane = lax.broadcasted_iota(jnp.int32, (BAND, LANES), 1)
    for sb in range(tq // BAND):
        rows = slice(sb * BAND, (sb + 1) * BAND)
        keys = slice(sb * BAND, (sb + 2) * BAND)
        mask = band0 if sb == 0 else band
        q = q_ref[rows, :]
        m_t = jnp.zeros((BAND, LANES), F32)
        l_t = jnp.ones((BAND, LANES), F32)
        for pr in range(N_HEADS // 2):
            vv = vx_ref[keys, 2 * pr * PAIR:(2 * pr + 2) * PAIR]
            pair = None
            for hh in range(2):
                h = 2 * pr + hh
                s = jnp.dot(q[:, pr * PAIR:(pr + 1) * PAIR], kx_ref[h * PAIR:(h + 1) * PAIR, keys],
                            preferred_element_type=F32)
                s = jnp.where(mask, s, NEG)
                m = jnp.max(s, axis=-1, keepdims=True)
                p = jnp.exp2(s - m)
                r = jnp.dot(p.astype(BF), vv, preferred_element_type=F32)
                l = r[:, PAIR:]
                contrib = r[:, :PAIR] / l
                pair = contrib if hh == 0 else jnp.where(lane < HEAD_DIM, pair, contrib)
                m_t = jnp.where(lane == h, m, m_t)
                l_t = jnp.where(lane == h, l, l_t)
            o_ref[rows, pr * PAIR:(pr + 1) * PAIR] = pair.astype(o_ref.dtype)
        lse_ref[rows, :] = m_t * LN2 + jnp.log(l_t)


def _attention_prompt(q, k, v, g):
    B, d, L, _ = q.shape
    assert WINDOWS[g] // d == BAND and L % BAND == 0
    tq = min(512, L)
    assert L % tq == 0
    sub = tq // BAND
    rb = min(d, 512 // tq)
    cur = pl.BlockSpec((None, rb, tq, ATTN_W), lambda b, r, j: (b, r, j, 0))
    prev = pl.BlockSpec((None, rb, BAND, ATTN_W), lambda b, r, j: (b, r, jnp.maximum(j * sub - 1, 0), 0))
    return pl.pallas_call(
        _attn_kernel,
        grid=(B, d // rb, L // tq),
        in_specs=[cur, cur, prev, cur, prev],
        out_specs=[cur, pl.BlockSpec((None, rb, tq, LANES), lambda b, r, j: (b, r, j, 0))],
        out_shape=[jax.ShapeDtypeStruct((B, d, L, ATTN_W), BF), jax.ShapeDtypeStruct((B, d, L, LANES), F32)],
        scratch_shapes=[pltpu.VMEM((N_HEADS * PAIR, tq + BAND), BF), pltpu.VMEM((tq + BAND, 2 * ATTN_W), BF)],
        compiler_params=_params(3),
        name="attn_g%d" % (g + 1),
    )(q, k, k, v, v)


def _ln_swish(c, g, b):
    mu = jnp.mean(c, axis=-1, keepdims=True)
    cc = c - mu
    var = jnp.mean(cc * cc, axis=-1, keepdims=True)
    y = cc * lax.rsqrt(var + EPS) * g + b
    return y * jax.nn.sigmoid(y)


def _conv_kernel(u_ref, halo_ref, w_ref, b_ref, lg_ref, lb_ref, o_ref, ext_ref, cv_ref):
    tc = u_ref.shape[0]
    rc = 64
    n = tc + HALO
    ext_ref[0, 0:HALO, :] = jnp.where(pl.program_id(1) > 0, halo_ref[...].astype(F32), 0.0)
    ext_ref[0, HALO:n, :] = u_ref[...].astype(F32)
    for s in range(1, SUBLANES):
        ext_ref[s, 0:n - SUBLANES, :] = ext_ref[0, s:n - SUBLANES + s, :]
    lead = HALO - (CONV_TAPS - 1)

    def row_chunk(i, carry):
        r0 = pl.multiple_of(i * rc, rc)
        for l0 in range(0, CONV_CH, LANES):
            ls = slice(l0, l0 + LANES)
            acc = jnp.broadcast_to(b_ref[:, ls], (rc, LANES))
            taps = w_ref[:, ls]
            for j in range(CONV_TAPS):
                a8, s = divmod(lead + j, SUBLANES)
                acc = acc + taps[j:j + 1, :] * ext_ref[s, pl.ds(r0 + a8 * SUBLANES, rc), ls]
            cv_ref[pl.ds(r0, rc), ls] = acc
        o_ref[pl.ds(r0, rc), :] = _ln_swish(cv_ref[pl.ds(r0, rc), :], lg_ref[...], lb_ref[...]).astype(o_ref.dtype)
        return carry

    lax.fori_loop(0, tc // rc, row_chunk, 0)


def _conv_prompt(u, B, S, conv_w, conv_b, ln_g, ln_b, tc):
    nt = S // tc
    assert S % tc == 0 and tc % HALO == 0
    hb = tc // HALO
    vec = _resident((1, CONV_CH))
    return pl.pallas_call(
        _conv_kernel,
        grid=(B, nt),
        in_specs=[pl.BlockSpec((tc, CONV_CH), lambda b, i: (b * nt + i, 0)),
                  pl.BlockSpec((HALO, CONV_CH), lambda b, i: (jnp.maximum((b * nt + i) * hb - 1, 0), 0)),
                  _resident((CONV_TAPS, CONV_CH)), vec, vec, vec],
        out_specs=pl.BlockSpec((tc, CONV_CH), lambda b, i: (b * nt + i, 0)),
        out_shape=jax.ShapeDtypeStruct((B * S, CONV_CH), BF),
        scratch_shapes=[pltpu.VMEM((SUBLANES, tc + HALO, CONV_CH), F32), pltpu.VMEM((tc, CONV_CH), F32)],
        compiler_params=_params(2),
        name="conv_prompt",
    )(u, u, conv_w, conv_b, ln_g, ln_b)


def _sample_conv_kernel(st_ref, un_ref, w_ref, b_ref, lg_ref, lb_ref, ca_ref, so_ref):
    n = CONV_TAPS - 1
    un = un_ref[...]
    acc = b_ref[...] + w_ref[n:n + 1, :] * un
    for j in range(n):
        acc = acc + w_ref[j:j + 1, :] * st_ref[j]
    ca_ref[...] = _ln_swish(acc, lg_ref[...], lb_ref[...]).astype(ca_ref.dtype)
    so_ref[0:n - 1] = st_ref[1:n]
    so_ref[n - 1] = un


def _conv_sample(state, u_new, conv_w, conv_b, ln_g, ln_b, bb):
    n, nb, _ = state.shape
    assert n == CONV_TAPS - 1 and nb % bb == 0
    vec = _resident((1, CONV_CH))
    return pl.pallas_call(
        _sample_conv_kernel,
        grid=(nb // bb,),
        in_specs=[pl.BlockSpec((n, bb, CONV_CH), lambda i: (0, i, 0)),
                  pl.BlockSpec((bb, CONV_CH), lambda i: (i, 0)),
                  _resident((CONV_TAPS, CONV_CH)), vec, vec, vec],
        out_specs=[pl.BlockSpec((bb, CONV_CH), lambda i: (i, 0)),
                   pl.BlockSpec((n, bb, CONV_CH), lambda i: (0, i, 0))],
        out_shape=[jax.ShapeDtypeStruct((nb, CONV_CH), BF), jax.ShapeDtypeStruct(state.shape, F32)],
        compiler_params=_params(1),
        name="conv_sample",
    )(state, u_new, conv_w, conv_b, ln_g, ln_b)


def _sample_attn_kernel(qt_ref, knt_ref, vnt_ref, k1, v1, k2, v2, k3, v3,
                        o_ref, ok1, ov1, ok2, ov2, ok3, ov3, bc_ref):
    for i, src in enumerate((qt_ref, knt_ref, vnt_ref)):
        for cl in range(N_GROUPS * HEAD_CHUNK):
            bc_ref[i, :, cl * LANES:(cl + 1) * LANES] = jnp.broadcast_to(src[:, cl:cl + 1], (HEAD_DIM, LANES))
    col = lambda i, cl: bc_ref[i, :, cl * LANES:(cl + 1) * LANES]
    kin, vin = (k1, k2, k3), (v1, v2, v3)
    kout, vout = (ok1, ok2, ok3), (ov1, ov2, ov3)
    head_rows = [slice(hh * HEAD_DIM, (hh + 1) * HEAD_DIM) for hh in range(HEAD_CHUNK)]
    scores, new_scores, m = [], [], None
    for g in range(N_GROUPS):
        lb = kin[g].shape[1]
        rows_s, rows_n = [], []
        for hh in range(HEAD_CHUNK):
            cl = g * HEAD_CHUNK + hh
            qc = col(0, cl)
            rows_s.append(jnp.sum(kin[g][head_rows[hh], :] * jnp.concatenate([qc] * (lb // LANES), axis=1),
                                  axis=0, keepdims=True))
            rows_n.append(jnp.sum(col(1, cl) * qc, axis=0, keepdims=True)[:, 0:1])
        s = jnp.concatenate(rows_s, axis=0)
        sn = jnp.concatenate(rows_n, axis=0)
        if DILATIONS[g] > 1:
            pos = lax.broadcasted_iota(jnp.int32, (HEAD_CHUNK, lb), 1)
            s = jnp.where((pos & (DILATIONS[g] - 1)) == 0, s, NEG)
        mg = jnp.maximum(jnp.max(s, axis=1, keepdims=True), sn)
        m = mg if m is None else jnp.maximum(m, mg)
        scores.append(s)
        new_scores.append(sn)
    probs = [jnp.exp2(s - m) for s in scores]
    new_probs = [jnp.exp2(sn - m) for sn in new_scores]
    l = None
    for g in range(N_GROUPS):
        t = jnp.sum(probs[g], axis=1, keepdims=True) + new_probs[g]
        l = t if l is None else l + t
    inv = 1.0 / l
    o_tile = jnp.zeros((HEAD_DIM, HEAD_CHUNK), F32)
    o_lane = lax.broadcasted_iota(jnp.int32, (HEAD_DIM, HEAD_CHUNK), 1)
    for hh in range(HEAD_CHUNK):
        acc = None
        for g in range(N_GROUPS):
            cl = g * HEAD_CHUNK + hh
            t = (jnp.sum(vin[g][head_rows[hh], :] * probs[g][hh:hh + 1, :], axis=1, keepdims=True)
                 + col(2, cl)[:, 0:1] * new_probs[g][hh:hh + 1, :])
            acc = t if acc is None else acc + t
        o_tile = jnp.where(o_lane == hh, acc * inv[hh:hh + 1, :], o_tile)
    o_ref[...] = o_tile
    last = lax.broadcasted_iota(jnp.int32, (HEAD_DIM, LANES), 1) == LANES - 1

    def shift_in(dst, src, rows, new_col):
        lb = src.shape[1]
        rolled = pltpu.roll(src[rows, :], lb - 1, 1)
        if lb > LANES:
            dst[rows, 0:lb - LANES] = rolled[:, 0:lb - LANES]
        dst[rows, lb - LANES:lb] = jnp.where(last, new_col, rolled[:, lb - LANES:lb])

    for g in range(N_GROUPS):
        for hh in range(HEAD_CHUNK):
            cl = g * HEAD_CHUNK + hh
            shift_in(kout[g], kin[g], head_rows[hh], col(1, cl))
            shift_in(vout[g], vin[g], head_rows[hh], col(2, cl))


def _columns(t):
    n = t.shape[0]
    t = t.reshape(n, N_GROUPS, N_HEADS // HEAD_CHUNK, HEAD_CHUNK, HEAD_DIM)
    return t.transpose(0, 2, 4, 1, 3).reshape(n, N_HEADS // HEAD_CHUNK, HEAD_DIM, N_GROUPS * HEAD_CHUNK)


def _attention_sample(q, k_new, v_new, caches):
    nb = q.shape[0]
    nch = N_HEADS // HEAD_CHUNK
    rows = HEAD_CHUNK * HEAD_DIM
    for g, c in enumerate(caches):
        assert c.shape == (nb, ATTN_W, WINDOWS[g // 2]), "window buffers are expected full"
    col_spec = pl.BlockSpec((None, None, HEAD_DIM, N_GROUPS * HEAD_CHUNK), lambda b, h: (b, h, 0, 0))
    buf_specs = [pl.BlockSpec((None, rows, c.shape[2]), lambda b, h: (b, h, 0)) for c in caches]
    outs = pl.pallas_call(
        _sample_attn_kernel,
        grid=(nb, nch),
        in_specs=[col_spec] * 3 + buf_specs,
        out_specs=[pl.BlockSpec((None, None, HEAD_DIM, HEAD_CHUNK), lambda b, h: (b, h, 0, 0))] + buf_specs,
        out_shape=[jax.ShapeDtypeStruct((nb, nch, HEAD_DIM, HEAD_CHUNK), F32)]
                  + [jax.ShapeDtypeStruct(c.shape, F32) for c in caches],
        scratch_shapes=[pltpu.VMEM((3, HEAD_DIM, N_GROUPS * HEAD_CHUNK * LANES), F32)],
        compiler_params=_params(2),
        name="sample_attn",
    )(_columns(q), _columns(k_new), _columns(v_new), *caches)
    o = outs[0].transpose(0, 1, 3, 2).reshape(nb, ATTN_W)
    return o, outs[1:]


def _main_kernel(combine, x_ref, *refs):
    if combine:
        o_refs, l_refs, e_ref = refs[0:3], refs[3:6], refs[6]
        refs = refs[7:]
        ca_ref, sg_ref, wa_ref, wc_ref, wo_ref, g2_ref, w1_ref, w2_ref, y_ref, hid_ref, il_ref = refs
    else:
        o_ref = refs[0]
        ca_ref, sg_ref, wa_ref, wc_ref, wo_ref, g2_ref, w1_ref, w2_ref, y_ref, hid_ref = refs[1:]
    tm = x_ref.shape[0]
    if combine:
        slot = [0]

        def token_major(ref, g):
            d = DILATIONS[g]
            if d == 1:
                return ref[...].astype(F32)
            nch = ref.shape[-1] // LANES
            base = slot[0]
            slot[0] += nch
            for r in range(d):
                t = ref[r].astype(F32)
                for c in range(nch):
                    il_ref[base + c, pl.ds(r, tm // d, stride=d), :] = t[:, c * LANES:(c + 1) * LANES]
            return jnp.concatenate([il_ref[base + c] for c in range(nch)], axis=-1)

        ls = [token_major(l_refs[g], g) for g in range(N_GROUPS)]
        m = jnp.maximum(jnp.maximum(ls[0], ls[1]), ls[2])
        es = [jnp.exp(t - m) for t in ls]
        inv = 1.0 / (es[0] + es[1] + es[2])
        o = None
        for g in range(N_GROUPS):
            alpha = es[g] * inv
            hi = alpha.astype(BF)
            lo = (alpha - hi.astype(F32)).astype(BF)
            wide = (jnp.dot(hi, e_ref[...], preferred_element_type=F32)
                    + jnp.dot(lo, e_ref[...], preferred_element_type=F32))
            t = wide * token_major(o_refs[g], g)
            o = t if o is None else o + t
    else:
        o = o_ref[...]
    a = jnp.dot(o.astype(BF), wa_ref[...], preferred_element_type=F32)
    c = jnp.dot(ca_ref[...], wc_ref[...], preferred_element_type=F32)
    merged = sg_ref[:, :D_MODEL].astype(F32) * a + sg_ref[:, D_MODEL:].astype(F32) * c
    h = x_ref[...] + jnp.dot(merged.astype(BF), wo_ref[...], preferred_element_type=F32)
    hn = (h * lax.rsqrt(jnp.mean(h * h, axis=-1, keepdims=True) + EPS) * g2_ref[...]).astype(BF)
    fch = 512
    for c0 in range(0, D_FF, fch):
        t = jnp.maximum(jnp.dot(hn, w1_ref[:, c0:c0 + fch], preferred_element_type=F32), 0.0)
        hid_ref[:, c0:c0 + fch] = (t * t).astype(BF)
    y_ref[...] = h + jnp.dot(hid_ref[...], w2_ref[...], preferred_element_type=F32)


def _main(x2d, attn, ca, sg, weights, tm, nt=1):
    M = x2d.shape[0]
    assert M % (tm * nt) == 0
    B = M // (tm * nt)
    row = lambda width: pl.BlockSpec((tm, width), lambda b, i: (b * nt + i, 0))
    combine = isinstance(attn, tuple)
    scratch = [pltpu.VMEM((tm, D_FF), BF)]
    if combine:
        os_, ls_ = attn

        def group_spec(d, width):
            if d == 1:
                return row(width)
            return pl.BlockSpec((None, d, tm // d, width), lambda b, i: (b, 0, i, 0))

        head = jnp.arange(ATTN_W) // HEAD_DIM
        spread = (jnp.arange(LANES)[:, None] == head[None, :]).astype(BF)
        attn_args = list(os_) + list(ls_) + [spread]
        attn_specs = ([group_spec(d, ATTN_W) for d in DILATIONS] + [group_spec(d, LANES) for d in DILATIONS]
                      + [_resident((LANES, ATTN_W))])
        n_il = sum((ATTN_W + LANES) // LANES for d in DILATIONS if d > 1)
        scratch.append(pltpu.VMEM((n_il, tm, LANES), F32))
    else:
        attn_args, attn_specs = [attn], [row(ATTN_W)]
    return pl.pallas_call(
        functools.partial(_main_kernel, combine),
        grid=(B, nt),
        in_specs=[row(D_MODEL)] + attn_specs + [row(CONV_CH), row(2 * D_MODEL)]
                 + [_resident(w.shape) for w in weights],
        out_specs=row(D_MODEL),
        out_shape=jax.ShapeDtypeStruct((M, D_MODEL), F32),
        scratch_shapes=scratch,
        compiler_params=_params(2),
        name="main_prompt" if combine else "main_sample",
    )(x2d, *attn_args, ca, sg, *weights)


def kernel(x_prompt, x_sample, cache_k1, cache_v1, cache_k2, cache_v2, cache_k3, cache_v3, state_conv,
           norm1_g, w_in, q_norm_g, k_norm_g, conv_w, conv_b, conv_ln_g, conv_ln_b,
           w_attn_out, w_conv_out, w_o, norm2_g, w_ff1, w_ff2):
    assert w_in.shape[0] == 1, "one layer"
    B, S, _ = x_prompt.shape
    nb, ns, _ = x_sample.shape
    assert ns == 1, "one new token per sample row"
    tm = 512
    consts = _inproj_consts(norm1_g[0], w_in[0], q_norm_g[0], k_norm_g[0])
    conv_args = (conv_w[0], conv_b, conv_ln_g, conv_ln_b)
    weights = (w_attn_out[0].astype(BF), w_conv_out[0].astype(BF), w_o[0].astype(BF), norm2_g,
               w_ff1[0].astype(BF), w_ff2[0].astype(BF))

    outs = _inproj_prompt(x_prompt, consts, tm)
    qs, ks, vs = outs[0:3], outs[3:6], outs[6:9]
    u, sg = outs[9:11]
    kts, vts, ut = outs[11:14], outs[14:17], outs[17]
    attn_o, attn_l = [], []
    for g, d in enumerate(DILATIONS):
        as4 = lambda t: t.reshape(B, d, S // d, ATTN_W)
        o, lse = _attention_prompt(as4(qs[g]), as4(ks[g]), as4(vs[g]), g)
        attn_o.append(o.reshape(B * S, ATTN_W) if d == 1 else o)
        attn_l.append(lse.reshape(B * S, LANES) if d == 1 else lse)
    ca = _conv_prompt(u, B, S, *conv_args, tc=tm)
    y_prompt = _main(x_prompt.reshape(B * S, D_MODEL), (tuple(attn_o), tuple(attn_l)), ca, sg, weights,
                     tm=tm, nt=S // tm).reshape(B, S, D_MODEL)
    tail = lambda t: t.reshape(1, B, t.shape[1], N_HEADS, HEAD_DIM)
    p_conv = ut[None, :, HALO - (CONV_TAPS - 1):, :]

    q_s, k_s, v_s, u_s, sg_s = _inproj_sample(x_sample.reshape(nb, D_MODEL), consts)
    to_buf = lambda c: c[0].transpose(0, 2, 3, 1).reshape(nb, ATTN_W, c.shape[2])
    from_buf = lambda c: c.reshape(nb, N_HEADS, HEAD_DIM, c.shape[2]).transpose(0, 3, 1, 2)[None]
    o_s, bufs = _attention_sample(q_s, k_s, v_s, [to_buf(c) for c in
                                                  (cache_k1, cache_v1, cache_k2, cache_v2, cache_k3, cache_v3)])
    ca_s, state_new = _conv_sample(state_conv[0].transpose(1, 0, 2), u_s, *conv_args, bb=min(32, nb))
    y_sample = _main(x_sample.reshape(nb, D_MODEL), o_s, ca_s, sg_s, weights, tm=nb).reshape(nb, 1, D_MODEL)
    s_conv = state_new.transpose(1, 0, 2)[None]

    kv_tails = [tail(t) for pair in zip(kts, vts) for t in pair]
    return (y_prompt, y_sample, *kv_tails, p_conv, *(from_buf(b) for b in bufs), s_conv)
```

```python
import functools
import math

import jax
import jax.numpy as jnp
from jax import lax
from jax.experimental import pallas as pl
from jax.experimental.pallas import tpu as pltpu

F32, BF = jnp.float32, jnp.bfloat16

D_MODEL = 1024
N_GROUPS = 3
WINDOWS = (128, 512, 2048)
DILATIONS = (1, 4, 16)
N_HEADS = 8
HEAD_DIM = 64
ATTN_W = N_HEADS * HEAD_DIM
QKV_W = N_GROUPS * ATTN_W
CONV_CH = D_MODEL
CONV_TAPS = 31
D_FF = 4 * D_MODEL
IN_W = 3 * QKV_W + 2 * CONV_CH + 2 * D_MODEL
EPS = 1e-6
LOG2E = math.log2(math.e)
LN2 = math.log(2.0)
Q_SCALE = LOG2E / 8.0
BAND = 128
NEG = -1e30

LANES = 128
SUBLANES = 8
PAIR = 2 * HEAD_DIM
HALO = 32
PCH = 512
NORM_W = 256
HEAD_CHUNK = 8
VMEM_LIMIT = 60 << 20


def _params(n_axes, vmem=VMEM_LIMIT):
    return pltpu.CompilerParams(dimension_semantics=("arbitrary",) * n_axes, vmem_limit_bytes=vmem)


def _resident(shape):
    nd = len(shape)
    return pl.BlockSpec(shape, lambda *_: (0,) * nd, pipeline_mode=pl.Buffered(1))


def _inproj_kernel(prompt, x_ref, g1_ref, w_ref, gq_ref, gk_ref, p_ref, *refs):
    if prompt:
        qs, ks, vs = refs[0:3], refs[3:6], refs[6:9]
        u_ref, sg_ref = refs[9:11]
        kts, vts = refs[11:14], refs[14:17]
        ut_ref, stage_ref = refs[17:19]
    else:
        q_ref, k_ref, v_ref, u_ref, sg_ref = refs
    tm = x_ref.shape[0]
    x = x_ref[...]
    xn = (x * lax.rsqrt(jnp.mean(x * x, axis=-1, keepdims=True) + EPS) * g1_ref[...]).astype(BF)

    def proj(c0):
        return jnp.dot(xn, w_ref[:, c0:c0 + PCH], preferred_element_type=F32)

    def head_norm(t, gain):
        parts = []
        for c0 in range(0, PCH, NORM_W):
            th = t[:, c0:c0 + NORM_W]
            ms = jnp.dot((th * th).astype(BF), p_ref[...], preferred_element_type=F32)
            parts.append(th * lax.rsqrt(ms + EPS) * gain[:, c0:c0 + NORM_W])
        return jnp.concatenate(parts, axis=-1)

    slot = [0]

    def put(dst, g, off, t):
        d = DILATIONS[g]
        if d == 1:
            dst[:, off:off + PCH] = t.astype(dst.dtype)
            return
        base = slot[0] % stage_ref.shape[0]
        slot[0] += PCH // LANES
        for c in range(PCH // LANES):
            stage_ref[base + c] = t[:, c * LANES:(c + 1) * LANES]
        for r in range(d):
            parts = [stage_ref[base + c, pl.ds(r, tm // d, stride=d), :] for c in range(PCH // LANES)]
            dst[r, :, off:off + PCH] = jnp.concatenate(parts, axis=-1).astype(dst.dtype)

    for c in range(QKV_W // PCH):
        cs = slice(c * PCH, (c + 1) * PCH)
        g, off = divmod(c * PCH, ATTN_W)
        qn = head_norm(proj(c * PCH), gq_ref[:, cs])
        kn = head_norm(proj(QKV_W + c * PCH), gk_ref[:, cs])
        vv = proj(2 * QKV_W + c * PCH)
        if prompt:
            put(qs[g], g, off, qn)
            put(ks[g], g, off, kn)
            put(vs[g], g, off, vv)
            rows = kts[g].shape[0]
            kts[g][:, off:off + PCH] = kn[tm - rows:, :]
            vts[g][:, off:off + PCH] = vv[tm - rows:, :]
        else:
            q_ref[:, cs] = qn
            k_ref[:, cs] = kn
            v_ref[:, cs] = vv
    for c in range(CONV_CH // PCH):
        cs = slice(c * PCH, (c + 1) * PCH)
        u = proj(3 * QKV_W + c * PCH) * jax.nn.sigmoid(proj(3 * QKV_W + CONV_CH + c * PCH))
        u_ref[:, cs] = u.astype(u_ref.dtype)
        if prompt:
            ut_ref[:, cs] = u[tm - HALO:, :]
    for c in range(2 * D_MODEL // PCH):
        cs = slice(c * PCH, (c + 1) * PCH)
        sg_ref[:, cs] = jax.nn.sigmoid(proj(3 * QKV_W + 2 * CONV_CH + c * PCH)).astype(sg_ref.dtype)


def _inproj_consts(norm1_g, w_in, q_norm_g, k_norm_g):
    gq = (jnp.repeat(q_norm_g, N_HEADS, axis=0) * Q_SCALE).reshape(1, QKV_W)
    gk = jnp.repeat(k_norm_g, N_HEADS, axis=0).reshape(1, QKV_W)
    head = jnp.arange(NORM_W) // HEAD_DIM
    pmat = jnp.where(head[:, None] == head[None, :], 1.0 / HEAD_DIM, 0.0).astype(BF)
    return norm1_g.reshape(1, D_MODEL), w_in.astype(BF), gq, gk, pmat


def _const_specs():
    return [_resident((1, D_MODEL)), _resident((D_MODEL, IN_W)), _resident((1, QKV_W)),
            _resident((1, QKV_W)), _resident((NORM_W, NORM_W))]


def _inproj_prompt(x, consts, tm):
    B, S, _ = x.shape
    nt = S // tm
    assert S % tm == 0 and S >= max(WINDOWS) and tm >= HALO
    assert all(tm % (2 * SUBLANES * d) == 0 for d in DILATIONS), "a residue's rows must fill bf16 tiles"
    row = lambda width: pl.BlockSpec((tm, width), lambda b, i: (b * nt + i, 0))

    def group_spec(d):
        if d == 1:
            return row(ATTN_W)
        return pl.BlockSpec((None, d, tm // d, ATTN_W), lambda b, i: (b, 0, i, 0))

    def group_shape(d):
        return jax.ShapeDtypeStruct((B * S, ATTN_W) if d == 1 else (B, d, S // d, ATTN_W), BF)

    def tail_spec(w):
        if w >= tm:
            assert w % tm == 0
            first = (S - w) // tm
            return pl.BlockSpec((None, tm, ATTN_W), lambda b, i: (b, jnp.maximum(i - first, 0), 0))
        assert tm % w == 0
        return pl.BlockSpec((None, w, ATTN_W), lambda b, i: (b, 0, 0))

    groups = [group_spec(d) for d in DILATIONS]
    group_shapes = [group_shape(d) for d in DILATIONS]
    tails = [tail_spec(w) for w in WINDOWS]
    tail_shapes = [jax.ShapeDtypeStruct((B, w, ATTN_W), F32) for w in WINDOWS]
    n_stage = 2 * (PCH // LANES)
    M = B * S
    return pl.pallas_call(
        functools.partial(_inproj_kernel, True),
        grid=(B, nt),
        in_specs=[row(D_MODEL)] + _const_specs(),
        out_specs=groups * 3 + [row(CONV_CH), row(2 * D_MODEL)] + tails + tails
                  + [pl.BlockSpec((None, HALO, CONV_CH), lambda b, i: (b, 0, 0))],
        out_shape=group_shapes * 3
                  + [jax.ShapeDtypeStruct((M, CONV_CH), BF), jax.ShapeDtypeStruct((M, 2 * D_MODEL), BF)]
                  + tail_shapes + tail_shapes + [jax.ShapeDtypeStruct((B, HALO, CONV_CH), F32)],
        scratch_shapes=[pltpu.VMEM((n_stage, tm, LANES), F32)],
        compiler_params=_params(2),
        name="inproj_prompt",
    )(x.reshape(M, D_MODEL), *consts)


def _inproj_sample(x2d, consts):
    M = x2d.shape[0]
    full = lambda width: pl.BlockSpec((M, width), lambda i: (0, 0))
    return pl.pallas_call(
        functools.partial(_inproj_kernel, False),
        grid=(1,),
        in_specs=[full(D_MODEL)] + _const_specs(),
        out_specs=[full(QKV_W), full(QKV_W), full(QKV_W), full(CONV_CH), full(2 * D_MODEL)],
        out_shape=[jax.ShapeDtypeStruct((M, QKV_W), F32)] * 3
                  + [jax.ShapeDtypeStruct((M, CONV_CH), F32), jax.ShapeDtypeStruct((M, 2 * D_MODEL), BF)],
        compiler_params=_params(1),
        name="inproj_sample",
    )(x2d, *consts)


def _attn_kernel(q_ref, kc_ref, kp_ref, vc_ref, vp_ref, o_ref, lse_ref, kx_ref, vx_ref):
    for ri in range(q_ref.shape[0]):
        _attn_residue(q_ref.at[ri], kc_ref.at[ri], kp_ref.at[ri], vc_ref.at[ri], vp_ref.at[ri],
                      o_ref.at[ri], lse_ref.at[ri], kx_ref, vx_ref)


def _attn_residue(q_ref, kc_ref, kp_ref, vc_ref, vp_ref, o_ref, lse_ref, kx_ref, vx_ref):
    tq = q_ref.shape[0]
    not_first = pl.program_id(2) > 0
    kt = jnp.concatenate([kp_ref[...].T, kc_ref[...].T], axis=1)
    zero = jnp.zeros((HEAD_DIM, tq + BAND), BF)
    for h in range(N_HEADS):
        kh = kt[h * HEAD_DIM:(h + 1) * HEAD_DIM]
        lo, hi = (kh, zero) if h % 2 == 0 else (zero, kh)
        kx_ref[h * PAIR:h * PAIR + HEAD_DIM, :] = lo
        kx_ref[h * PAIR + HEAD_DIM:(h + 1) * PAIR, :] = hi
    ones = jnp.ones((tq + BAND, PAIR), BF)
    for pr in range(N_HEADS // 2):
        vx_ref[0:BAND, 2 * pr * PAIR:(2 * pr + 1) * PAIR] = vp_ref[:, pr * PAIR:(pr + 1) * PAIR]
        vx_ref[BAND:, 2 * pr * PAIR:(2 * pr + 1) * PAIR] = vc_ref[:, pr * PAIR:(pr + 1) * PAIR]
        vx_ref[:, (2 * pr + 1) * PAIR:(2 * pr + 2) * PAIR] = ones
    a = lax.broadcasted_iota(jnp.int32, (BAND, 2 * BAND), 0)
    c = lax.broadcasted_iota(jnp.int32, (BAND, 2 * BAND), 1)
    band = jnp.logical_and(c >= a, c <= a + BAND)
    band0 = jnp.logical_and(band, jnp.logical_or(c >= BAND, not_first))
    lane = lax.broadcasted_iota(jnp.int32, (BAND, LANES), 1)
    for sb in range(tq // BAND):
        rows = slice(sb * BAND, (sb + 1) * BAND)
        keys = slice(sb * BAND, (sb + 2) * BAND)
        mask = band0 if sb == 0 else band
        q = q_ref[rows, :]
        m_t = jnp.zeros((BAND, LANES), F32)
        l_t = jnp.ones((BAND, LANES), F32)
        for pr in range(N_HEADS // 2):
            vv = vx_ref[keys, 2 * pr * PAIR:(2 * pr + 2) * PAIR]
            pair = None
            for hh in range(2):
                h = 2 * pr + hh
                s = jnp.dot(q[:, pr * PAIR:(pr + 1) * PAIR], kx_ref[h * PAIR:(h + 1) * PAIR, keys],
                            preferred_element_type=F32)
                s = jnp.where(mask, s, NEG)
                m = jnp.max(s, axis=-1, keepdims=True)
                p = jnp.exp2(s - m)
                r = jnp.dot(p.astype(BF), vv, preferred_element_type=F32)
                l = r[:, PAIR:]
                contrib = r[:, :PAIR] / l
                pair = contrib if hh == 0 else jnp.where(lane < HEAD_DIM, pair, contrib)
                m_t = jnp.where(lane == h, m, m_t)
                l_t = jnp.where(lane == h, l, l_t)
            o_ref[rows, pr * PAIR:(pr + 1) * PAIR] = pair.astype(o_ref.dtype)
        lse_ref[rows, :] = m_t * LN2 + jnp.log(l_t)


def _attention_prompt(q, k, v, g):
    B, d, L, _ = q.shape
    assert WINDOWS[g] // d == BAND and L % BAND == 0
    tq = min(512, L)
    assert L % tq == 0
    sub = tq // BAND
    rb = min(d, 512 // tq)
    cur = pl.BlockSpec((None, rb, tq, ATTN_W), lambda b, r, j: (b, r, j, 0))
    prev = pl.BlockSpec((None, rb, BAND, ATTN_W), lambda b, r, j: (b, r, jnp.maximum(j * sub - 1, 0), 0))
    return pl.pallas_call(
        _attn_kernel,
        grid=(B, d // rb, L // tq),
        in_specs=[cur, cur, prev, cur, prev],
        out_specs=[cur, pl.BlockSpec((None, rb, tq, LANES), lambda b, r, j: (b, r, j, 0))],
        out_shape=[jax.ShapeDtypeStruct((B, d, L, ATTN_W), BF), jax.ShapeDtypeStruct((B, d, L, LANES), F32)],
        scratch_shapes=[pltpu.VMEM((N_HEADS * PAIR, tq + BAND), BF), pltpu.VMEM((tq + BAND, 2 * ATTN_W), BF)],
        compiler_params=_params(3),
        name="attn_g%d" % (g + 1),
    )(q, k, k, v, v)


def _ln_swish(c, g, b):
    mu = jnp.mean(c, axis=-1, keepdims=True)
    cc = c - mu
    var = jnp.mean(cc * cc, axis=-1, keepdims=True)
    y = cc * lax.rsqrt(var + EPS) * g + b
    return y * jax.nn.sigmoid(y)


def _conv_tile(has_context, u_ref, halo_ref, w_ref, b_ref, lg_ref, lb_ref, o_ref, ext_ref, cv_ref):
    tc = u_ref.shape[0]
    rc = 64
    n = tc + HALO
    ext_ref[0, 0:HALO, :] = jnp.where(has_context, halo_ref[...].astype(F32), 0.0)
    ext_ref[0, HALO:n, :] = u_ref[...].astype(F32)
    for s in range(1, SUBLANES):
        ext_ref[s, 0:n - SUBLANES, :] = ext_ref[0, s:n - SUBLANES + s, :]
    lead = HALO - (CONV_TAPS - 1)

    def row_chunk(i, carry):
        r0 = pl.multiple_of(i * rc, rc)
        for l0 in range(0, CONV_CH, LANES):
            ls = slice(l0, l0 + LANES)
            acc = jnp.broadcast_to(b_ref[:, ls], (rc, LANES))
            taps = w_ref[:, ls]
            for j in range(CONV_TAPS):
                a8, s = divmod(lead + j, SUBLANES)
                acc = acc + taps[j:j + 1, :] * ext_ref[s, pl.ds(r0 + a8 * SUBLANES, rc), ls]
            cv_ref[pl.ds(r0, rc), ls] = acc
        o_ref[pl.ds(r0, rc), :] = _ln_swish(cv_ref[pl.ds(r0, rc), :], lg_ref[...], lb_ref[...]).astype(o_ref.dtype)
        return carry

    lax.fori_loop(0, tc // rc, row_chunk, 0)


def _conv_kernel(*refs):
    _conv_tile(pl.program_id(1) > 0, *refs)


def _conv_specs(tc, step):
    hb = tc // HALO
    vec = _resident((1, CONV_CH))
    in_specs = [pl.BlockSpec((tc, CONV_CH), lambda *ids: (step(*ids), 0)),
                pl.BlockSpec((HALO, CONV_CH), lambda *ids: (jnp.maximum(step(*ids) * hb - 1, 0), 0)),
                _resident((CONV_TAPS, CONV_CH)), vec, vec, vec]
    out_spec = pl.BlockSpec((tc, CONV_CH), lambda *ids: (step(*ids), 0))
    scratch = [pltpu.VMEM((SUBLANES, tc + HALO, CONV_CH), F32), pltpu.VMEM((tc, CONV_CH), F32)]
    return in_specs, out_spec, scratch


def _conv_prompt(u, B, S, conv_w, conv_b, ln_g, ln_b, tc):
    nt = S // tc
    assert S % tc == 0 and tc % HALO == 0
    in_specs, out_spec, scratch = _conv_specs(tc, lambda b, i: b * nt + i)
    return pl.pallas_call(
        _conv_kernel,
        grid=(B, nt),
        in_specs=in_specs,
        out_specs=out_spec,
        out_shape=jax.ShapeDtypeStruct((B * S, CONV_CH), BF),
        scratch_shapes=scratch,
        compiler_params=_params(2),
        name="conv_prompt",
    )(u, u, conv_w, conv_b, ln_g, ln_b)


def _sample_conv_kernel(st_ref, un_ref, w_ref, b_ref, lg_ref, lb_ref, ca_ref, so_ref):
    n = CONV_TAPS - 1
    un = un_ref[...]
    acc = b_ref[...] + w_ref[n:n + 1, :] * un
    for j in range(n):
        acc = acc + w_ref[j:j + 1, :] * st_ref[j]
    ca_ref[...] = _ln_swish(acc, lg_ref[...], lb_ref[...]).astype(ca_ref.dtype)
    so_ref[0:n - 1] = st_ref[1:n]
    so_ref[n - 1] = un


def _conv_sample(state, u_new, conv_w, conv_b, ln_g, ln_b, bb):
    n, nb, _ = state.shape
    assert n == CONV_TAPS - 1 and nb % bb == 0
    vec = _resident((1, CONV_CH))
    return pl.pallas_call(
        _sample_conv_kernel,
        grid=(nb // bb,),
        in_specs=[pl.BlockSpec((n, bb, CONV_CH), lambda i: (0, i, 0)),
                  pl.BlockSpec((bb, CONV_CH), lambda i: (i, 0)),
                  _resident((CONV_TAPS, CONV_CH)), vec, vec, vec],
        out_specs=[pl.BlockSpec((bb, CONV_CH), lambda i: (i, 0)),
                   pl.BlockSpec((n, bb, CONV_CH), lambda i: (0, i, 0))],
        out_shape=[jax.ShapeDtypeStruct((nb, CONV_CH), BF), jax.ShapeDtypeStruct(state.shape, F32)],
        compiler_params=_params(1),
        name="conv_sample",
    )(state, u_new, conv_w, conv_b, ln_g, ln_b)


def _sample_attn_kernel(qt_ref, knt_ref, vnt_ref, k1, v1, k2, v2, k3, v3,
                        o_ref, ok1, ov1, ok2, ov2, ok3, ov3, bc_ref):
    for i, src in enumerate((qt_ref, knt_ref, vnt_ref)):
        for cl in range(N_GROUPS * HEAD_CHUNK):
            bc_ref[i, :, cl * LANES:(cl + 1) * LANES] = jnp.broadcast_to(src[:, cl:cl + 1], (HEAD_DIM, LANES))
    col = lambda i, cl: bc_ref[i, :, cl * LANES:(cl + 1) * LANES]
    kin, vin = (k1, k2, k3), (v1, v2, v3)
    kout, vout = (ok1, ok2, ok3), (ov1, ov2, ov3)
    head_rows = [slice(hh * HEAD_DIM, (hh + 1) * HEAD_DIM) for hh in range(HEAD_CHUNK)]
    scores, new_scores, m = [], [], None
    for g in range(N_GROUPS):
        lb = kin[g].shape[1]
        rows_s, rows_n = [], []
        for hh in range(HEAD_CHUNK):
            cl = g * HEAD_CHUNK + hh
            qc = col(0, cl)
            rows_s.append(jnp.sum(kin[g][head_rows[hh], :] * jnp.concatenate([qc] * (lb // LANES), axis=1),
                                  axis=0, keepdims=True))
            rows_n.append(jnp.sum(col(1, cl) * qc, axis=0, keepdims=True)[:, 0:1])
        s = jnp.concatenate(rows_s, axis=0)
        sn = jnp.concatenate(rows_n, axis=0)
        if DILATIONS[g] > 1:
            pos = lax.broadcasted_iota(jnp.int32, (HEAD_CHUNK, lb), 1)
            s = jnp.where((pos & (DILATIONS[g] - 1)) == 0, s, NEG)
        mg = jnp.maximum(jnp.max(s, axis=1, keepdims=True), sn)
        m = mg if m is None else jnp.maximum(m, mg)
        scores.append(s)
        new_scores.append(sn)
    probs = [jnp.exp2(s - m) for s in scores]
    new_probs = [jnp.exp2(sn - m) for sn in new_scores]
    l = None
    for g in range(N_GROUPS):
        t = jnp.sum(probs[g], axis=1, keepdims=True) + new_probs[g]
        l = t if l is None else l + t
    inv = 1.0 / l
    o_tile = jnp.zeros((HEAD_DIM, HEAD_CHUNK), F32)
    o_lane = lax.broadcasted_iota(jnp.int32, (HEAD_DIM, HEAD_CHUNK), 1)
    for hh in range(HEAD_CHUNK):
        acc = None
        for g in range(N_GROUPS):
            cl = g * HEAD_CHUNK + hh
            t = (jnp.sum(vin[g][head_rows[hh], :] * probs[g][hh:hh + 1, :], axis=1, keepdims=True)
                 + col(2, cl)[:, 0:1] * new_probs[g][hh:hh + 1, :])
            acc = t if acc is None else acc + t
        o_tile = jnp.where(o_lane == hh, acc * inv[hh:hh + 1, :], o_tile)
    o_ref[...] = o_tile
    last = lax.broadcasted_iota(jnp.int32, (HEAD_DIM, LANES), 1) == LANES - 1

    def shift_in(dst, src, rows, new_col):
        lb = src.shape[1]
        rolled = pltpu.roll(src[rows, :], lb - 1, 1)
        if lb > LANES:
            dst[rows, 0:lb - LANES] = rolled[:, 0:lb - LANES]
        dst[rows, lb - LANES:lb] = jnp.where(last, new_col, rolled[:, lb - LANES:lb])

    for g in range(N_GROUPS):
        for hh in range(HEAD_CHUNK):
            cl = g * HEAD_CHUNK + hh
            shift_in(kout[g], kin[g], head_rows[hh], col(1, cl))
            shift_in(vout[g], vin[g], head_rows[hh], col(2, cl))


def _columns(t):
    n = t.shape[0]
    t = t.reshape(n, N_GROUPS, N_HEADS // HEAD_CHUNK, HEAD_CHUNK, HEAD_DIM)
    return t.transpose(0, 2, 4, 1, 3).reshape(n, N_HEADS // HEAD_CHUNK, HEAD_DIM, N_GROUPS * HEAD_CHUNK)


N_SAMPLE_IN, N_SAMPLE_OUT, N_CONV_IN = 9, 7, 6


def _sample_attn_conv_kernel(tiles_per_seq, *refs):
    a, b, c = N_SAMPLE_IN, N_SAMPLE_IN + N_CONV_IN, N_SAMPLE_IN + N_CONV_IN + N_SAMPLE_OUT
    bc_ref, ext_ref, cv_ref = refs[c + 1:]
    _sample_attn_kernel(*refs[0:a], *refs[b:c], bc_ref)
    step = pl.program_id(0) * pl.num_programs(1) + pl.program_id(1)
    _conv_tile(step % tiles_per_seq > 0, *refs[a:b], refs[c], ext_ref, cv_ref)


def _attention_sample(q, k_new, v_new, caches, conv=None):
    nb = q.shape[0]
    nch = N_HEADS // HEAD_CHUNK
    rows = HEAD_CHUNK * HEAD_DIM
    for g, c in enumerate(caches):
        assert c.shape == (nb, ATTN_W, WINDOWS[g // 2]), "window buffers are expected full"
    col_spec = pl.BlockSpec((None, None, HEAD_DIM, N_GROUPS * HEAD_CHUNK), lambda b, h: (b, h, 0, 0))
    buf_specs = [pl.BlockSpec((None, rows, c.shape[2]), lambda b, h: (b, h, 0)) for c in caches]
    in_specs = [col_spec] * 3 + buf_specs
    out_specs = [pl.BlockSpec((None, None, HEAD_DIM, HEAD_CHUNK), lambda b, h: (b, h, 0, 0))] + buf_specs
    out_shape = ([jax.ShapeDtypeStruct((nb, nch, HEAD_DIM, HEAD_CHUNK), F32)]
                 + [jax.ShapeDtypeStruct(c.shape, F32) for c in caches])
    scratch = [pltpu.VMEM((3, HEAD_DIM, N_GROUPS * HEAD_CHUNK * LANES), F32)]
    args = [_columns(q), _columns(k_new), _columns(v_new), *caches]
    body, name = _sample_attn_kernel, "sample_attn"
    fuse = False
    if conv is not None:
        u, S = conv[0], conv[1]
        tc, rem = divmod(u.shape[0], nb * nch)
        fuse = rem == 0 and tc > 0 and S % tc == 0 and tc % 64 == 0
    if fuse:
        c_in, c_out, c_scratch = _conv_specs(tc, lambda b, h: b * nch + h)
        in_specs, out_specs, scratch = in_specs + c_in, out_specs + [c_out], scratch + c_scratch
        out_shape = out_shape + [jax.ShapeDtypeStruct(u.shape, BF)]
        args = args + [u, u, *conv[2:]]
        body, name = functools.partial(_sample_attn_conv_kernel, S // tc), "sample_attn_conv"
    outs = pl.pallas_call(
        body,
        grid=(nb, nch),
        in_specs=in_specs,
        out_specs=out_specs,
        out_shape=out_shape,
        scratch_shapes=scratch,
        compiler_params=_params(2),
        name=name,
    )(*args)
    o = outs[0].transpose(0, 1, 3, 2).reshape(nb, ATTN_W)
    return o, outs[1:1 + len(caches)], (outs[-1] if fuse else None)


def _main_kernel(combine, x_ref, *refs):
    if combine:
        o_refs, l_refs, e_ref = refs[0:3], refs[3:6], refs[6]
        refs = refs[7:]
        ca_ref, sg_ref, wa_ref, wc_ref, wo_ref, g2_ref, w1_ref, w2_ref, y_ref, hid_ref, il_ref = refs
    else:
        o_ref = refs[0]
        ca_ref, sg_ref, wa_ref, wc_ref, wo_ref, g2_ref, w1_ref, w2_ref, y_ref, hid_ref = refs[1:]
    tm = x_ref.shape[0]
    if combine:
        slot = [0]

        def token_major(ref, g):
            d = DILATIONS[g]
            if d == 1:
                return ref[...].astype(F32)
            nch = ref.shape[-1] // LANES
            base = slot[0]
            slot[0] += nch
            for r in range(d):
                t = ref[r].astype(F32)
                for c in range(nch):
                    il_ref[base + c, pl.ds(r, tm // d, stride=d), :] = t[:, c * LANES:(c + 1) * LANES]
            return jnp.concatenate([il_ref[base + c] for c in range(nch)], axis=-1)

        ls = [token_major(l_refs[g], g) for g in range(N_GROUPS)]
        m = jnp.maximum(jnp.maximum(ls[0], ls[1]), ls[2])
        es = [jnp.exp(t - m) for t in ls]
        inv = 1.0 / (es[0] + es[1] + es[2])
        o = None
        for g in range(N_GROUPS):
            alpha = es[g] * inv
            hi = alpha.astype(BF)
            lo = (alpha - hi.astype(F32)).astype(BF)
            wide = (jnp.dot(hi, e_ref[...], preferred_element_type=F32)
                    + jnp.dot(lo, e_ref[...], preferred_element_type=F32))
            t = wide * token_major(o_refs[g], g)
            o = t if o is None else o + t
    else:
        o = o_ref[...]
    a = jnp.dot(o.astype(BF), wa_ref[...], preferred_element_type=F32)
    c = jnp.dot(ca_ref[...], wc_ref[...], preferred_element_type=F32)
    merged = sg_ref[:, :D_MODEL].astype(F32) * a + sg_ref[:, D_MODEL:].astype(F32) * c
    h = x_ref[...] + jnp.dot(merged.astype(BF), wo_ref[...], preferred_element_type=F32)
    hn = (h * lax.rsqrt(jnp.mean(h * h, axis=-1, keepdims=True) + EPS) * g2_ref[...]).astype(BF)
    fch = 512
    for c0 in range(0, D_FF, fch):
        t = jnp.maximum(jnp.dot(hn, w1_ref[:, c0:c0 + fch], preferred_element_type=F32), 0.0)
        hid_ref[:, c0:c0 + fch] = (t * t).astype(BF)
    y_ref[...] = h + jnp.dot(hid_ref[...], w2_ref[...], preferred_element_type=F32)


def _main(x2d, attn, ca, sg, weights, tm, nt=1):
    M = x2d.shape[0]
    assert M % (tm * nt) == 0
    B = M // (tm * nt)
    row = lambda width: pl.BlockSpec((tm, width), lambda b, i: (b * nt + i, 0))
    combine = isinstance(attn, tuple)
    scratch = [pltpu.VMEM((tm, D_FF), BF)]
    if combine:
        os_, ls_ = attn

        def group_spec(d, width):
            if d == 1:
                return row(width)
            return pl.BlockSpec((None, d, tm // d, width), lambda b, i: (b, 0, i, 0))

        head = jnp.arange(ATTN_W) // HEAD_DIM
        spread = (jnp.arange(LANES)[:, None] == head[None, :]).astype(BF)
        attn_args = list(os_) + list(ls_) + [spread]
        attn_specs = ([group_spec(d, ATTN_W) for d in DILATIONS] + [group_spec(d, LANES) for d in DILATIONS]
                      + [_resident((LANES, ATTN_W))])
        n_il = sum((ATTN_W + LANES) // LANES for d in DILATIONS if d > 1)
        scratch.append(pltpu.VMEM((n_il, tm, LANES), F32))
    else:
        attn_args, attn_specs = [attn], [row(ATTN_W)]
    return pl.pallas_call(
        functools.partial(_main_kernel, combine),
        grid=(B, nt),
        in_specs=[row(D_MODEL)] + attn_specs + [row(CONV_CH), row(2 * D_MODEL)]
                 + [_resident(w.shape) for w in weights],
        out_specs=row(D_MODEL),
        out_shape=jax.ShapeDtypeStruct((M, D_MODEL), F32),
        scratch_shapes=scratch,
        compiler_params=_params(2),
        name="main_prompt" if combine else "main_sample",
    )(x2d, *attn_args, ca, sg, *weights)


def kernel(x_prompt, x_sample, cache_k1, cache_v1, cache_k2, cache_v2, cache_k3, cache_v3, state_conv,
           norm1_g, w_in, q_norm_g, k_norm_g, conv_w, conv_b, conv_ln_g, conv_ln_b,
           w_attn_out, w_conv_out, w_o, norm2_g, w_ff1, w_ff2):
    assert w_in.shape[0] == 1, "one layer"
    B, S, _ = x_prompt.shape
    nb, ns, _ = x_sample.shape
    assert ns == 1, "one new token per sample row"
    tm = 512
    consts = _inproj_consts(norm1_g[0], w_in[0], q_norm_g[0], k_norm_g[0])
    conv_args = (conv_w[0], conv_b, conv_ln_g, conv_ln_b)
    weights = (w_attn_out[0].astype(BF), w_conv_out[0].astype(BF), w_o[0].astype(BF), norm2_g,
               w_ff1[0].astype(BF), w_ff2[0].astype(BF))

    outs = _inproj_prompt(x_prompt, consts, tm)
    qs, ks, vs = outs[0:3], outs[3:6], outs[6:9]
    u, sg = outs[9:11]
    kts, vts, ut = outs[11:14], outs[14:17], outs[17]
    attn_o, attn_l = [], []
    for g, d in enumerate(DILATIONS):
        as4 = lambda t: t.reshape(B, d, S // d, ATTN_W)
        o, lse = _attention_prompt(as4(qs[g]), as4(ks[g]), as4(vs[g]), g)
        attn_o.append(o.reshape(B * S, ATTN_W) if d == 1 else o)
        attn_l.append(lse.reshape(B * S, LANES) if d == 1 else lse)
    tail = lambda t: t.reshape(1, B, t.shape[1], N_HEADS, HEAD_DIM)
    p_conv = ut[None, :, HALO - (CONV_TAPS - 1):, :]

    q_s, k_s, v_s, u_s, sg_s = _inproj_sample(x_sample.reshape(nb, D_MODEL), consts)
    to_buf = lambda c: c[0].transpose(0, 2, 3, 1).reshape(nb, ATTN_W, c.shape[2])
    from_buf = lambda c: c.reshape(nb, N_HEADS, HEAD_DIM, c.shape[2]).transpose(0, 3, 1, 2)[None]
    o_s, bufs, ca = _attention_sample(q_s, k_s, v_s, [to_buf(c) for c in
                                                      (cache_k1, cache_v1, cache_k2, cache_v2, cache_k3, cache_v3)],
                                      conv=(u, S, *conv_args))
    if ca is None:
        ca = _conv_prompt(u, B, S, *conv_args, tc=tm)
    y_prompt = _main(x_prompt.reshape(B * S, D_MODEL), (tuple(attn_o), tuple(attn_l)), ca, sg, weights,
                     tm=tm, nt=S // tm).reshape(B, S, D_MODEL)
    ca_s, state_new = _conv_sample(state_conv[0].transpose(1, 0, 2), u_s, *conv_args, bb=min(32, nb))
    y_sample = _main(x_sample.reshape(nb, D_MODEL), o_s, ca_s, sg_s, weights, tm=nb).reshape(nb, 1, D_MODEL)
    s_conv = state_new.transpose(1, 0, 2)[None]

    kv_tails = [tail(t) for pair in zip(kts, vts) for t in pair]
    return (y_prompt, y_sample, *kv_tails, p_conv, *(from_buf(b) for b in bufs), s_conv)
```

```python
import functools
import math

import jax
import jax.numpy as jnp
from jax import lax
from jax.experimental import pallas as pl
from jax.experimental.pallas import tpu as pltpu

F32, BF = jnp.float32, jnp.bfloat16

D_MODEL = 1024
N_GROUPS = 3
WINDOWS = (128, 512, 2048)
DILATIONS = (1, 4, 16)
N_HEADS = 8
HEAD_DIM = 64
ATTN_W = N_HEADS * HEAD_DIM
QKV_W = N_GROUPS * ATTN_W
CONV_CH = D_MODEL
CONV_TAPS = 31
D_FF = 4 * D_MODEL
IN_W = 3 * QKV_W + 2 * CONV_CH + 2 * D_MODEL
EPS = 1e-6
LOG2E = math.log2(math.e)
LN2 = math.log(2.0)
Q_SCALE = LOG2E / 8.0
BAND = 128
NEG = -1e30

LANES = 128
SUBLANES = 8
PAIR = 2 * HEAD_DIM
HALO = 32
PCH = 512
NORM_W = 256
HEAD_CHUNK = 8
VMEM_LIMIT = 60 << 20


def _params(n_axes, vmem=VMEM_LIMIT):
    return pltpu.CompilerParams(dimension_semantics=("arbitrary",) * n_axes, vmem_limit_bytes=vmem)


def _resident(shape):
    nd = len(shape)
    return pl.BlockSpec(shape, lambda *_: (0,) * nd, pipeline_mode=pl.Buffered(1))


def _inproj_kernel(prompt, x_ref, g1_ref, w_ref, gq_ref, gk_ref, p_ref, *refs):
    if prompt:
        qs, ks, vs = refs[0:3], refs[3:6], refs[6:9]
        u_ref, sg_ref = refs[9:11]
        kts, vts = refs[11:14], refs[14:17]
        ut_ref, stage_ref = refs[17:19]
    else:
        q_ref, k_ref, v_ref, u_ref, sg_ref = refs
    tm = x_ref.shape[0]
    x = x_ref[...]
    xn = (x * lax.rsqrt(jnp.mean(x * x, axis=-1, keepdims=True) + EPS) * g1_ref[...]).astype(BF)

    def proj(c0):
        return jnp.dot(xn, w_ref[:, c0:c0 + PCH], preferred_element_type=F32)

    def head_norm(t, gain):
        parts = []
        for c0 in range(0, PCH, NORM_W):
            th = t[:, c0:c0 + NORM_W]
            ms = jnp.dot((th * th).astype(BF), p_ref[...], preferred_element_type=F32)
            parts.append(th * lax.rsqrt(ms + EPS) * gain[:, c0:c0 + NORM_W])
        return jnp.concatenate(parts, axis=-1)

    slot = [0]

    def put(dst, g, off, t):
        d = DILATIONS[g]
        if d == 1:
            dst[:, off:off + PCH] = t.astype(dst.dtype)
            return
        base = slot[0] % stage_ref.shape[0]
        slot[0] += PCH // LANES
        for c in range(PCH // LANES):
            stage_ref[base + c] = t[:, c * LANES:(c + 1) * LANES]
        for r in range(d):
            parts = [stage_ref[base + c, pl.ds(r, tm // d, stride=d), :] for c in range(PCH // LANES)]
            dst[r, :, off:off + PCH] = jnp.concatenate(parts, axis=-1).astype(dst.dtype)

    for c in range(QKV_W // PCH):
        cs = slice(c * PCH, (c + 1) * PCH)
        g, off = divmod(c * PCH, ATTN_W)
        qn = head_norm(proj(c * PCH), gq_ref[:, cs])
        kn = head_norm(proj(QKV_W + c * PCH), gk_ref[:, cs])
        vv = proj(2 * QKV_W + c * PCH)
        if prompt:
            put(qs[g], g, off, qn)
            put(ks[g], g, off, kn)
            put(vs[g], g, off, vv)
            rows = kts[g].shape[0]
            kts[g][:, off:off + PCH] = kn[tm - rows:, :]
            vts[g][:, off:off + PCH] = vv[tm - rows:, :]
        else:
            q_ref[:, cs] = qn
            k_ref[:, cs] = kn
            v_ref[:, cs] = vv
    for c in range(CONV_CH // PCH):
        cs = slice(c * PCH, (c + 1) * PCH)
        u = proj(3 * QKV_W + c * PCH) * jax.nn.sigmoid(proj(3 * QKV_W + CONV_CH + c * PCH))
        u_ref[:, cs] = u.astype(u_ref.dtype)
        if prompt:
            ut_ref[:, cs] = u[tm - HALO:, :]
    for c in range(2 * D_MODEL // PCH):
        cs = slice(c * PCH, (c + 1) * PCH)
        sg_ref[:, cs] = jax.nn.sigmoid(proj(3 * QKV_W + 2 * CONV_CH + c * PCH)).astype(sg_ref.dtype)


def _inproj_consts(norm1_g, w_in, q_norm_g, k_norm_g):
    gq = (jnp.repeat(q_norm_g, N_HEADS, axis=0) * Q_SCALE).reshape(1, QKV_W)
    gk = jnp.repeat(k_norm_g, N_HEADS, axis=0).reshape(1, QKV_W)
    head = jnp.arange(NORM_W) // HEAD_DIM
    pmat = jnp.where(head[:, None] == head[None, :], 1.0 / HEAD_DIM, 0.0).astype(BF)
    return norm1_g.reshape(1, D_MODEL), w_in.astype(BF), gq, gk, pmat


def _const_specs():
    return [_resident((1, D_MODEL)), _resident((D_MODEL, IN_W)), _resident((1, QKV_W)),
            _resident((1, QKV_W)), _resident((NORM_W, NORM_W))]


def _inproj_prompt(x, consts, tm):
    B, S, _ = x.shape
    nt = S // tm
    assert S % tm == 0 and S >= max(WINDOWS) and tm >= HALO
    assert all(tm % (2 * SUBLANES * d) == 0 for d in DILATIONS), "a residue's rows must fill bf16 tiles"
    row = lambda width: pl.BlockSpec((tm, width), lambda b, i: (b * nt + i, 0))

    def group_spec(d):
        if d == 1:
            return row(ATTN_W)
        return pl.BlockSpec((None, d, tm // d, ATTN_W), lambda b, i: (b, 0, i, 0))

    def group_shape(d):
        return jax.ShapeDtypeStruct((B * S, ATTN_W) if d == 1 else (B, d, S // d, ATTN_W), BF)

    def tail_spec(w):
        if w >= tm:
            assert w % tm == 0
            first = (S - w) // tm
            return pl.BlockSpec((None, tm, ATTN_W), lambda b, i: (b, jnp.maximum(i - first, 0), 0))
        assert tm % w == 0
        return pl.BlockSpec((None, w, ATTN_W), lambda b, i: (b, 0, 0))

    groups = [group_spec(d) for d in DILATIONS]
    group_shapes = [group_shape(d) for d in DILATIONS]
    tails = [tail_spec(w) for w in WINDOWS]
    tail_shapes = [jax.ShapeDtypeStruct((B, w, ATTN_W), F32) for w in WINDOWS]
    n_stage = 2 * (PCH // LANES)
    M = B * S
    return pl.pallas_call(
        functools.partial(_inproj_kernel, True),
        grid=(B, nt),
        in_specs=[row(D_MODEL)] + _const_specs(),
        out_specs=groups * 3 + [row(CONV_CH), row(2 * D_MODEL)] + tails + tails
                  + [pl.BlockSpec((None, HALO, CONV_CH), lambda b, i: (b, 0, 0))],
        out_shape=group_shapes * 3
                  + [jax.ShapeDtypeStruct((M, CONV_CH), BF), jax.ShapeDtypeStruct((M, 2 * D_MODEL), BF)]
                  + tail_shapes + tail_shapes + [jax.ShapeDtypeStruct((B, HALO, CONV_CH), F32)],
        scratch_shapes=[pltpu.VMEM((n_stage, tm, LANES), F32)],
        compiler_params=_params(2),
        name="inproj_prompt",
    )(x.reshape(M, D_MODEL), *consts)


def _inproj_sample(x2d, consts):
    M = x2d.shape[0]
    full = lambda width: pl.BlockSpec((M, width), lambda i: (0, 0))
    return pl.pallas_call(
        functools.partial(_inproj_kernel, False),
        grid=(1,),
        in_specs=[full(D_MODEL)] + _const_specs(),
        out_specs=[full(QKV_W), full(QKV_W), full(QKV_W), full(CONV_CH), full(2 * D_MODEL)],
        out_shape=[jax.ShapeDtypeStruct((M, QKV_W), F32)] * 3
                  + [jax.ShapeDtypeStruct((M, CONV_CH), F32), jax.ShapeDtypeStruct((M, 2 * D_MODEL), BF)],
        compiler_params=_params(1),
        name="inproj_sample",
    )(x2d, *consts)


def _attn_kernel(q_ref, kc_ref, kp_ref, vc_ref, vp_ref, o_ref, lse_ref, kx_ref, vx_ref):
    for ri in range(q_ref.shape[0]):
        _attn_residue(q_ref.at[ri], kc_ref.at[ri], kp_ref.at[ri], vc_ref.at[ri], vp_ref.at[ri],
                      o_ref.at[ri], lse_ref.at[ri], kx_ref, vx_ref)


def _attn_residue(q_ref, kc_ref, kp_ref, vc_ref, vp_ref, o_ref, lse_ref, kx_ref, vx_ref):
    tq = q_ref.shape[0]
    not_first = pl.program_id(2) > 0
    kt = jnp.concatenate([kp_ref[...].T, kc_ref[...].T], axis=1)
    zero = jnp.zeros((HEAD_DIM, tq + BAND), BF)
    for h in range(N_HEADS):
        kh = kt[h * HEAD_DIM:(h + 1) * HEAD_DIM]
        lo, hi = (kh, zero) if h % 2 == 0 else (zero, kh)
        kx_ref[h * PAIR:h * PAIR + HEAD_DIM, :] = lo
        kx_ref[h * PAIR + HEAD_DIM:(h + 1) * PAIR, :] = hi
    ones = jnp.ones((tq + BAND, PAIR), BF)
    for pr in range(N_HEADS // 2):
        vx_ref[0:BAND, 2 * pr * PAIR:(2 * pr + 1) * PAIR] = vp_ref[:, pr * PAIR:(pr + 1) * PAIR]
        vx_ref[BAND:, 2 * pr * PAIR:(2 * pr + 1) * PAIR] = vc_ref[:, pr * PAIR:(pr + 1) * PAIR]
        vx_ref[:, (2 * pr + 1) * PAIR:(2 * pr + 2) * PAIR] = ones
    a = lax.broadcasted_iota(jnp.int32, (BAND, 2 * BAND), 0)
    c = lax.broadcasted_iota(jnp.int32, (BAND, 2 * BAND), 1)
    band = jnp.logical_and(c >= a, c <= a + BAND)
    band0 = jnp.logical_and(band, jnp.logical_or(c >= BAND, not_first))
    lane = lax.broadcasted_iota(jnp.int32, (BAND, LANES), 1)
    for sb in range(tq // BAND):
        rows = slice(sb * BAND, (sb + 1) * BAND)
        keys = slice(sb * BAND, (sb + 2) * BAND)
        mask = band0 if sb == 0 else band
        q = q_ref[rows, :]
        m_t = jnp.zeros((BAND, LANES), F32)
        l_t = jnp.ones((BAND, LANES), F32)
        for pr in range(N_HEADS // 2):
            vv = vx_ref[keys, 2 * pr * PAIR:(2 * pr + 2) * PAIR]
            pair = None
            for hh in range(2):
                h = 2 * pr + hh
                s = jnp.dot(q[:, pr * PAIR:(pr + 1) * PAIR], kx_ref[h * PAIR:(h + 1) * PAIR, keys],
                            preferred_element_type=F32)
                s = jnp.where(mask, s, NEG)
                m = jnp.max(s, axis=-1, keepdims=True)
                p = jnp.exp2(s - m)
                r = jnp.dot(p.astype(BF), vv, preferred_element_type=F32)
                l = r[:, PAIR:]
                contrib = r[:, :PAIR] / l
                pair = contrib if hh == 0 else jnp.where(lane < HEAD_DIM, pair, contrib)
                m_t = jnp.where(lane == h, m, m_t)
                l_t = jnp.where(lane == h, l, l_t)
            o_ref[rows, pr * PAIR:(pr + 1) * PAIR] = pair.astype(o_ref.dtype)
        lse_ref[rows, :] = m_t * LN2 + jnp.log(l_t)


def _attention_prompt(q, k, v, g):
    B, d, L, _ = q.shape
    assert WINDOWS[g] // d == BAND and L % BAND == 0
    tq = min(512, L)
    assert L % tq == 0
    sub = tq // BAND
    rb = min(d, 512 // tq)
    cur = pl.BlockSpec((None, rb, tq, ATTN_W), lambda b, r, j: (b, r, j, 0))
    prev = pl.BlockSpec((None, rb, BAND, ATTN_W), lambda b, r, j: (b, r, jnp.maximum(j * sub - 1, 0), 0))
    return pl.pallas_call(
        _attn_kernel,
        grid=(B, d // rb, L // tq),
        in_specs=[cur, cur, prev, cur, prev],
        out_specs=[cur, pl.BlockSpec((None, rb, tq, LANES), lambda b, r, j: (b, r, j, 0))],
        out_shape=[jax.ShapeDtypeStruct((B, d, L, ATTN_W), BF), jax.ShapeDtypeStruct((B, d, L, LANES), F32)],
        scratch_shapes=[pltpu.VMEM((N_HEADS * PAIR, tq + BAND), BF), pltpu.VMEM((tq + BAND, 2 * ATTN_W), BF)],
        compiler_params=_params(3),
        name="attn_g%d" % (g + 1),
    )(q, k, k, v, v)


def _ln_swish(c, g, b):
    mu = jnp.mean(c, axis=-1, keepdims=True)
    cc = c - mu
    var = jnp.mean(cc * cc, axis=-1, keepdims=True)
    y = cc * lax.rsqrt(var + EPS) * g + b
    return y * jax.nn.sigmoid(y)


CONV_ROWS = 64


def _conv_tile(has_context, u_ref, halo_ref, w_ref, b_ref, o_ref, ext_ref):
    tc = u_ref.shape[0]
    rc = CONV_ROWS
    n = tc + HALO
    ext_ref[0, 0:HALO, :] = jnp.where(has_context, halo_ref[...].astype(F32), 0.0)
    ext_ref[0, HALO:n, :] = u_ref[...].astype(F32)
    for s in range(1, SUBLANES):
        ext_ref[s, 0:n - SUBLANES, :] = ext_ref[0, s:n - SUBLANES + s, :]
    lead = HALO - (CONV_TAPS - 1)

    def row_chunk(i, carry):
        r0 = pl.multiple_of(i * rc, rc)
        for l0 in range(0, CONV_CH, LANES):
            ls = slice(l0, l0 + LANES)
            acc = jnp.broadcast_to(b_ref[:, ls], (rc, LANES))
            taps = w_ref[:, ls]
            for s in range(SUBLANES):
                js = [j for j in range(CONV_TAPS) if (lead + j) % SUBLANES == s]
                a_lo, a_hi = (lead + js[0]) // SUBLANES, (lead + js[-1]) // SUBLANES
                xs = ext_ref[s, pl.ds(r0 + a_lo * SUBLANES, rc + (a_hi - a_lo) * SUBLANES), ls]
                for j in js:
                    o8 = ((lead + j) // SUBLANES - a_lo) * SUBLANES
                    acc = acc + taps[j:j + 1, :] * xs[o8:o8 + rc]
            o_ref[pl.ds(r0, rc), ls] = acc.astype(o_ref.dtype)
        return carry

    lax.fori_loop(0, tc // rc, row_chunk, 0)


def _conv_kernel(*refs):
    _conv_tile(pl.program_id(1) > 0, *refs)


def _conv_specs(tc, step):
    hb = tc // HALO
    in_specs = [pl.BlockSpec((tc, CONV_CH), lambda *ids: (step(*ids), 0)),
                pl.BlockSpec((HALO, CONV_CH), lambda *ids: (jnp.maximum(step(*ids) * hb - 1, 0), 0)),
                _resident((CONV_TAPS, CONV_CH)), _resident((1, CONV_CH))]
    out_spec = pl.BlockSpec((tc, CONV_CH), lambda *ids: (step(*ids), 0))
    scratch = [pltpu.VMEM((SUBLANES, tc + HALO, CONV_CH), F32)]
    return in_specs, out_spec, scratch


def _conv_prompt(u, B, S, conv_w, conv_b, tc):
    nt = S // tc
    assert S % tc == 0 and tc % HALO == 0
    in_specs, out_spec, scratch = _conv_specs(tc, lambda b, i: b * nt + i)
    return pl.pallas_call(
        _conv_kernel,
        grid=(B, nt),
        in_specs=in_specs,
        out_specs=out_spec,
        out_shape=jax.ShapeDtypeStruct((B * S, CONV_CH), BF),
        scratch_shapes=scratch,
        compiler_params=_params(2),
        name="conv_prompt",
    )(u, u, conv_w, conv_b)


def _sample_conv_kernel(st_ref, un_ref, w_ref, b_ref, ca_ref, so_ref):
    n = CONV_TAPS - 1
    un = un_ref[...]
    acc = b_ref[...] + w_ref[n:n + 1, :] * un
    for j in range(n):
        acc = acc + w_ref[j:j + 1, :] * st_ref[j]
    ca_ref[...] = acc.astype(ca_ref.dtype)
    so_ref[0:n - 1] = st_ref[1:n]
    so_ref[n - 1] = un


def _conv_sample(state, u_new, conv_w, conv_b, bb):
    n, nb, _ = state.shape
    assert n == CONV_TAPS - 1 and nb % bb == 0
    return pl.pallas_call(
        _sample_conv_kernel,
        grid=(nb // bb,),
        in_specs=[pl.BlockSpec((n, bb, CONV_CH), lambda i: (0, i, 0)),
                  pl.BlockSpec((bb, CONV_CH), lambda i: (i, 0)),
                  _resident((CONV_TAPS, CONV_CH)), _resident((1, CONV_CH))],
        out_specs=[pl.BlockSpec((bb, CONV_CH), lambda i: (i, 0)),
                   pl.BlockSpec((n, bb, CONV_CH), lambda i: (0, i, 0))],
        out_shape=[jax.ShapeDtypeStruct((nb, CONV_CH), BF), jax.ShapeDtypeStruct(state.shape, F32)],
        compiler_params=_params(1),
        name="conv_sample",
    )(state, u_new, conv_w, conv_b)


def _sample_attn_kernel(qt_ref, knt_ref, vnt_ref, k1, v1, k2, v2, k3, v3,
                        o_ref, ok1, ov1, ok2, ov2, ok3, ov3, bc_ref):
    for i, src in enumerate((qt_ref, knt_ref, vnt_ref)):
        for cl in range(N_GROUPS * HEAD_CHUNK):
            bc_ref[i, cl] = jnp.broadcast_to(src[:, cl:cl + 1], (HEAD_DIM, LANES))
    col = lambda i, cl: bc_ref[i, cl]
    kin, vin = (k1, k2, k3), (v1, v2, v3)
    head_rows = [slice(hh * HEAD_DIM, (hh + 1) * HEAD_DIM) for hh in range(HEAD_CHUNK)]
    scores, new_scores, m = [], [], None
    for g in range(N_GROUPS):
        lb = kin[g].shape[1]
        rows_s, rows_n = [], []
        for hh in range(HEAD_CHUNK):
            cl = g * HEAD_CHUNK + hh
            qc = col(0, cl)
            rows_s.append(jnp.sum(kin[g][head_rows[hh], :] * jnp.concatenate([qc] * (lb // LANES), axis=1),
                                  axis=0, keepdims=True))
            rows_n.append(jnp.sum(col(1, cl) * qc, axis=0, keepdims=True)[:, 0:1])
        s = jnp.concatenate(rows_s, axis=0)
        sn = jnp.concatenate(rows_n, axis=0)
        if DILATIONS[g] > 1:
            pos = lax.broadcasted_iota(jnp.int32, (HEAD_CHUNK, lb), 1)
            s = jnp.where((pos & (DILATIONS[g] - 1)) == 0, s, NEG)
        mg = jnp.maximum(jnp.max(s, axis=1, keepdims=True), sn)
        m = mg if m is None else jnp.maximum(m, mg)
        scores.append(s)
        new_scores.append(sn)
    probs = [jnp.exp2(s - m) for s in scores]
    new_probs = [jnp.exp2(sn - m) for sn in new_scores]
    l = None
    for g in range(N_GROUPS):
        t = jnp.sum(probs[g], axis=1, keepdims=True) + new_probs[g]
        l = t if l is None else l + t
    inv = 1.0 / l
    o_tile = jnp.zeros((HEAD_DIM, HEAD_CHUNK), F32)
    o_lane = lax.broadcasted_iota(jnp.int32, (HEAD_DIM, HEAD_CHUNK), 1)
    for hh in range(HEAD_CHUNK):
        acc = None
        for g in range(N_GROUPS):
            cl = g * HEAD_CHUNK + hh
            t = (jnp.sum(vin[g][head_rows[hh], :] * probs[g][hh:hh + 1, :], axis=1, keepdims=True)
                 + col(2, cl)[:, 0:1] * new_probs[g][hh:hh + 1, :])
            acc = t if acc is None else acc + t
        o_tile = jnp.where(o_lane == hh, acc * inv[hh:hh + 1, :], o_tile)
    o_ref[...] = o_tile
    _shift_heads(kin, vin, (ok1, ok2, ok3), (ov1, ov2, ov3), bc_ref)


def _shift_heads(kin, vin, kout, vout, bc_ref):
    last = lax.broadcasted_iota(jnp.int32, (HEAD_DIM, LANES), 1) == LANES - 1

    def shift_in(dst, src, rows, new_col):
        lb = src.shape[1]
        rolled = pltpu.roll(src[rows, :], lb - 1, 1)
        if lb > LANES:
            dst[rows, 0:lb - LANES] = rolled[:, 0:lb - LANES]
        dst[rows, lb - LANES:lb] = jnp.where(last, new_col, rolled[:, lb - LANES:lb])

    for g in range(N_GROUPS):
        for head in range(HEAD_CHUNK):
            rows = slice(head * HEAD_DIM, (head + 1) * HEAD_DIM)
            cl = g * HEAD_CHUNK + head
            shift_in(kout[g], kin[g], rows, bc_ref[1, cl])
            shift_in(vout[g], vin[g], rows, bc_ref[2, cl])


def _columns(t):
    n = t.shape[0]
    t = t.reshape(n, N_GROUPS, N_HEADS // HEAD_CHUNK, HEAD_CHUNK, HEAD_DIM)
    return t.transpose(0, 2, 4, 1, 3).reshape(n, N_HEADS // HEAD_CHUNK, HEAD_DIM, N_GROUPS * HEAD_CHUNK)


N_SAMPLE_IN, N_SAMPLE_OUT, N_CONV_IN = 9, 7, 4


def _sample_attn_conv_kernel(tiles_per_seq, *refs):
    a, b, c = N_SAMPLE_IN, N_SAMPLE_IN + N_CONV_IN, N_SAMPLE_IN + N_CONV_IN + N_SAMPLE_OUT
    bc_ref, ext_ref = refs[c + 1:]
    _sample_attn_kernel(*refs[0:a], *refs[b:c], bc_ref)
    step = pl.program_id(0) * pl.num_programs(1) + pl.program_id(1)
    _conv_tile(step % tiles_per_seq > 0, *refs[a:b], refs[c], ext_ref)


def _attention_sample(q, k_new, v_new, caches, conv=None):
    nb = q.shape[0]
    nch = N_HEADS // HEAD_CHUNK
    rows = HEAD_CHUNK * HEAD_DIM
    for g, c in enumerate(caches):
        assert c.shape == (nb, ATTN_W, WINDOWS[g // 2]), "window buffers are expected full"
    col_spec = pl.BlockSpec((None, None, HEAD_DIM, N_GROUPS * HEAD_CHUNK), lambda b, h: (b, h, 0, 0))
    buf_specs = [pl.BlockSpec((None, rows, c.shape[2]), lambda b, h: (b, h, 0)) for c in caches]
    in_specs = [col_spec] * 3 + buf_specs
    out_specs = [pl.BlockSpec((None, None, HEAD_DIM, HEAD_CHUNK), lambda b, h: (b, h, 0, 0))] + buf_specs
    out_shape = ([jax.ShapeDtypeStruct((nb, nch, HEAD_DIM, HEAD_CHUNK), F32)]
                 + [jax.ShapeDtypeStruct(c.shape, F32) for c in caches])
    scratch = [pltpu.VMEM((3, N_GROUPS * HEAD_CHUNK, HEAD_DIM, LANES), F32)]
    args = [_columns(q), _columns(k_new), _columns(v_new), *caches]
    body, name = _sample_attn_kernel, "sample_attn"
    fuse = False
    if conv is not None:
        u, S = conv[0], conv[1]
        tc, rem = divmod(u.shape[0], nb * nch)
        fuse = rem == 0 and tc > 0 and S % tc == 0 and tc % 64 == 0
    if fuse:
        c_in, c_out, c_scratch = _conv_specs(tc, lambda b, h: b * nch + h)
        in_specs, out_specs, scratch = in_specs + c_in, out_specs + [c_out], scratch + c_scratch
        out_shape = out_shape + [jax.ShapeDtypeStruct(u.shape, BF)]
        args = args + [u, u, *conv[2:]]
        body, name = functools.partial(_sample_attn_conv_kernel, S // tc), "sample_attn_conv"
    outs = pl.pallas_call(
        body,
        grid=(nb, nch),
        in_specs=in_specs,
        out_specs=out_specs,
        out_shape=out_shape,
        scratch_shapes=scratch,
        compiler_params=_params(2),
        name=name,
    )(*args)
    o = outs[0].transpose(0, 1, 3, 2).reshape(nb, ATTN_W)
    return o, outs[1:1 + len(caches)], (outs[-1] if fuse else None)


def _main_kernel(combine, x_ref, *refs):
    if combine:
        o_refs, l_refs, e_ref = refs[0:3], refs[3:6], refs[6]
        refs = refs[7:]
        (cv_ref, sg_ref, lg_ref, lb_ref, wa_ref, wc_ref, wo_ref, g2_ref, w1_ref, w2_ref,
         y_ref, hid_ref, il_ref) = refs
    else:
        o_ref = refs[0]
        (cv_ref, sg_ref, lg_ref, lb_ref, wa_ref, wc_ref, wo_ref, g2_ref, w1_ref, w2_ref,
         y_ref, hid_ref) = refs[1:]
    tm = x_ref.shape[0]
    if combine:
        slot = [0]

        def token_major(ref, g):
            d = DILATIONS[g]
            if d == 1:
                return ref[...].astype(F32)
            nch = ref.shape[-1] // LANES
            base = slot[0]
            slot[0] += nch
            for r in range(d):
                t = ref[r].astype(F32)
                for c in range(nch):
                    il_ref[base + c, pl.ds(r, tm // d, stride=d), :] = t[:, c * LANES:(c + 1) * LANES]
            return jnp.concatenate([il_ref[base + c] for c in range(nch)], axis=-1)

        ls = [token_major(l_refs[g], g) for g in range(N_GROUPS)]
        m = jnp.maximum(jnp.maximum(ls[0], ls[1]), ls[2])
        es = [jnp.exp(t - m) for t in ls]
        inv = 1.0 / (es[0] + es[1] + es[2])
        o = None
        for g in range(N_GROUPS):
            alpha = es[g] * inv
            hi = alpha.astype(BF)
            lo = (alpha - hi.astype(F32)).astype(BF)
            wide = jnp.dot(jnp.concatenate([hi, lo], axis=-1), e_ref[...], preferred_element_type=F32)
            t = wide * token_major(o_refs[g], g)
            o = t if o is None else o + t
    else:
        o = o_ref[...]
    a = jnp.dot(o.astype(BF), wa_ref[...], preferred_element_type=F32)
    ca = _ln_swish(cv_ref[...].astype(F32), lg_ref[...], lb_ref[...]).astype(BF)
    c = jnp.dot(ca, wc_ref[...], preferred_element_type=F32)
    merged = sg_ref[:, :D_MODEL].astype(F32) * a + sg_ref[:, D_MODEL:].astype(F32) * c
    h = x_ref[...] + jnp.dot(merged.astype(BF), wo_ref[...], preferred_element_type=F32)
    hn = (h * lax.rsqrt(jnp.mean(h * h, axis=-1, keepdims=True) + EPS) * g2_ref[...]).astype(BF)
    fch = 512
    for c0 in range(0, D_FF, fch):
        t = jnp.maximum(jnp.dot(hn, w1_ref[:, c0:c0 + fch], preferred_element_type=F32), 0.0)
        hid_ref[:, c0:c0 + fch] = (t * t).astype(BF)
    y_ref[...] = h + jnp.dot(hid_ref[...], w2_ref[...], preferred_element_type=F32)


def _main(x2d, attn, ca, sg, weights, tm, nt=1):
    M = x2d.shape[0]
    assert M % (tm * nt) == 0
    B = M // (tm * nt)
    row = lambda width: pl.BlockSpec((tm, width), lambda b, i: (b * nt + i, 0))
    combine = isinstance(attn, tuple)
    scratch = [pltpu.VMEM((tm, D_FF), BF)]
    if combine:
        os_, ls_ = attn

        def group_spec(d, width):
            if d == 1:
                return row(width)
            return pl.BlockSpec((None, d, tm // d, width), lambda b, i: (b, 0, i, 0))

        head = jnp.arange(ATTN_W) // HEAD_DIM
        spread = (jnp.arange(LANES)[:, None] == head[None, :]).astype(BF)
        attn_args = list(os_) + list(ls_) + [jnp.concatenate([spread, spread], axis=0)]
        attn_specs = ([group_spec(d, ATTN_W) for d in DILATIONS] + [group_spec(d, LANES) for d in DILATIONS]
                      + [_resident((2 * LANES, ATTN_W))])
        n_il = sum((ATTN_W + LANES) // LANES for d in DILATIONS if d > 1)
        scratch.append(pltpu.VMEM((n_il, tm, LANES), F32))
    else:
        attn_args, attn_specs = [attn], [row(ATTN_W)]
    return pl.pallas_call(
        functools.partial(_main_kernel, combine),
        grid=(B, nt),
        in_specs=[row(D_MODEL)] + attn_specs + [row(CONV_CH), row(2 * D_MODEL)]
                 + [_resident(w.shape) for w in weights],
        out_specs=row(D_MODEL),
        out_shape=jax.ShapeDtypeStruct((M, D_MODEL), F32),
        scratch_shapes=scratch,
        compiler_params=_params(2),
        name="main_prompt" if combine else "main_sample",
    )(x2d, *attn_args, ca, sg, *weights)


def kernel(x_prompt, x_sample, cache_k1, cache_v1, cache_k2, cache_v2, cache_k3, cache_v3, state_conv,
           norm1_g, w_in, q_norm_g, k_norm_g, conv_w, conv_b, conv_ln_g, conv_ln_b,
           w_attn_out, w_conv_out, w_o, norm2_g, w_ff1, w_ff2):
    assert w_in.shape[0] == 1, "one layer"
    B, S, _ = x_prompt.shape
    nb, ns, _ = x_sample.shape
    assert ns == 1, "one new token per sample row"
    tm = 512
    consts = _inproj_consts(norm1_g[0], w_in[0], q_norm_g[0], k_norm_g[0])
    conv_args = (conv_w[0], conv_b)
    weights = (conv_ln_g, conv_ln_b, w_attn_out[0].astype(BF), w_conv_out[0].astype(BF), w_o[0].astype(BF),
               norm2_g, w_ff1[0].astype(BF), w_ff2[0].astype(BF))

    outs = _inproj_prompt(x_prompt, consts, tm)
    qs, ks, vs = outs[0:3], outs[3:6], outs[6:9]
    u, sg = outs[9:11]
    kts, vts, ut = outs[11:14], outs[14:17], outs[17]
    attn_o, attn_l = [], []
    for g, d in enumerate(DILATIONS):
        as4 = lambda t: t.reshape(B, d, S // d, ATTN_W)
        o, lse = _attention_prompt(as4(qs[g]), as4(ks[g]), as4(vs[g]), g)
        attn_o.append(o.reshape(B * S, ATTN_W) if d == 1 else o)
        attn_l.append(lse.reshape(B * S, LANES) if d == 1 else lse)
    tail = lambda t: t.reshape(1, B, t.shape[1], N_HEADS, HEAD_DIM)
    p_conv = ut[None, :, HALO - (CONV_TAPS - 1):, :]

    q_s, k_s, v_s, u_s, sg_s = _inproj_sample(x_sample.reshape(nb, D_MODEL), consts)
    to_buf = lambda c: c[0].transpose(0, 2, 3, 1).reshape(nb, ATTN_W, c.shape[2])
    from_buf = lambda c: c.reshape(nb, N_HEADS, HEAD_DIM, c.shape[2]).transpose(0, 3, 1, 2)[None]
    o_s, bufs, ca = _attention_sample(q_s, k_s, v_s, [to_buf(c) for c in
                                                      (cache_k1, cache_v1, cache_k2, cache_v2, cache_k3, cache_v3)],
                                      conv=(u, S, *conv_args))
    if ca is None:
        ca = _conv_prompt(u, B, S, *conv_args, tc=tm)
    y_prompt = _main(x_prompt.reshape(B * S, D_MODEL), (tuple(attn_o), tuple(attn_l)), ca, sg, weights,
                     tm=tm, nt=S // tm).reshape(B, S, D_MODEL)
    ca_s, state_new = _conv_sample(state_conv[0].transpose(1, 0, 2), u_s, *conv_args, bb=min(32, nb))
    y_sample = _main(x_sample.reshape(nb, D_MODEL), o_s, ca_s, sg_s, weights, tm=nb).reshape(nb, 1, D_MODEL)
    s_conv = state_new.transpose(1, 0, 2)[None]

    kv_tails = [tail(t) for pair in zip(kts, vts) for t in pair]
    return (y_prompt, y_sample, *kv_tails, p_conv, *(from_buf(b) for b in bufs), s_conv)
```

```python
import functools
import math

import jax
import jax.numpy as jnp
from jax import lax
from jax.experimental import pallas as pl
from jax.experimental.pallas import tpu as pltpu

F32, BF = jnp.float32, jnp.bfloat16

D_MODEL = 1024
N_GROUPS = 3
WINDOWS = (128, 512, 2048)
DILATIONS = (1, 4, 16)
N_HEADS = 8
HEAD_DIM = 64
ATTN_W = N_HEADS * HEAD_DIM
QKV_W = N_GROUPS * ATTN_W
CONV_CH = D_MODEL
CONV_TAPS = 31
D_FF = 4 * D_MODEL
IN_W = 3 * QKV_W + 2 * CONV_CH + 2 * D_MODEL
EPS = 1e-6
LOG2E = math.log2(math.e)
LN2 = math.log(2.0)
Q_SCALE = LOG2E / 8.0
BAND = 128
NEG = -1e30

LANES = 128
SUBLANES = 8
PAIR = 2 * HEAD_DIM
HALO = 32
PCH = 512
NORM_W = 256
HEAD_CHUNK = 8
VMEM_LIMIT = 60 << 20


def _params(n_axes, vmem=VMEM_LIMIT):
    return pltpu.CompilerParams(dimension_semantics=("arbitrary",) * n_axes, vmem_limit_bytes=vmem)


def _resident(shape):
    nd = len(shape)
    return pl.BlockSpec(shape, lambda *_: (0,) * nd, pipeline_mode=pl.Buffered(1))


def _inproj_kernel(prompt, x_ref, g1_ref, w_ref, gq_ref, gk_ref, p_ref, *refs):
    if prompt:
        qs, ks, vs = refs[0:3], refs[3:6], refs[6:9]
        u_ref, sg_ref = refs[9:11]
        kts, vts = refs[11:14], refs[14:17]
        ut_ref, stage_ref = refs[17:19]
    else:
        q_ref, k_ref, v_ref, u_ref, sg_ref = refs
    tm = x_ref.shape[0]
    x = x_ref[...]
    xn = (x * lax.rsqrt(jnp.mean(x * x, axis=-1, keepdims=True) + EPS) * g1_ref[...]).astype(BF)

    def proj(c0):
        return jnp.dot(xn, w_ref[:, c0:c0 + PCH], preferred_element_type=F32)

    def head_norm(t, gain):
        parts = []
        for c0 in range(0, PCH, NORM_W):
            th = t[:, c0:c0 + NORM_W]
            ms = jnp.dot((th * th).astype(BF), p_ref[...], preferred_element_type=F32)
            parts.append(th * lax.rsqrt(ms + EPS) * gain[:, c0:c0 + NORM_W])
        return jnp.concatenate(parts, axis=-1)

    slot = [0]

    def put(dst, g, off, t):
        d = DILATIONS[g]
        if d == 1:
            dst[:, off:off + PCH] = t.astype(dst.dtype)
            return
        base = slot[0] % stage_ref.shape[0]
        slot[0] += PCH // LANES
        for c in range(PCH // LANES):
            stage_ref[base + c] = t[:, c * LANES:(c + 1) * LANES]
        for r in range(d):
            parts = [stage_ref[base + c, pl.ds(r, tm // d, stride=d), :] for c in range(PCH // LANES)]
            dst[r, :, off:off + PCH] = jnp.concatenate(parts, axis=-1).astype(dst.dtype)

    for c in range(QKV_W // PCH):
        cs = slice(c * PCH, (c + 1) * PCH)
        g, off = divmod(c * PCH, ATTN_W)
        qn = head_norm(proj(c * PCH), gq_ref[:, cs])
        kn = head_norm(proj(QKV_W + c * PCH), gk_ref[:, cs])
        vv = proj(2 * QKV_W + c * PCH)
        if prompt:
            put(qs[g], g, off, qn)
            put(ks[g], g, off, kn)
            put(vs[g], g, off, vv)
            rows = kts[g].shape[0]
            kts[g][:, off:off + PCH] = kn[tm - rows:, :]
            vts[g][:, off:off + PCH] = vv[tm - rows:, :]
        else:
            q_ref[:, cs] = qn
            k_ref[:, cs] = kn
            v_ref[:, cs] = vv
    for c in range(CONV_CH // PCH):
        cs = slice(c * PCH, (c + 1) * PCH)
        u = proj(3 * QKV_W + c * PCH) * jax.nn.sigmoid(proj(3 * QKV_W + CONV_CH + c * PCH))
        u_ref[:, cs] = u.astype(u_ref.dtype)
        if prompt:
            ut_ref[:, cs] = u[tm - HALO:, :]
    for c in range(2 * D_MODEL // PCH):
        cs = slice(c * PCH, (c + 1) * PCH)
        sg_ref[:, cs] = jax.nn.sigmoid(proj(3 * QKV_W + 2 * CONV_CH + c * PCH)).astype(sg_ref.dtype)


def _inproj_consts(norm1_g, w_in, q_norm_g, k_norm_g):
    gq = (jnp.repeat(q_norm_g, N_HEADS, axis=0) * Q_SCALE).reshape(1, QKV_W)
    gk = jnp.repeat(k_norm_g, N_HEADS, axis=0).reshape(1, QKV_W)
    head = jnp.arange(NORM_W) // HEAD_DIM
    pmat = jnp.where(head[:, None] == head[None, :], 1.0 / HEAD_DIM, 0.0).astype(BF)
    return norm1_g.reshape(1, D_MODEL), w_in.astype(BF), gq, gk, pmat


def _const_specs():
    return [_resident((1, D_MODEL)), _resident((D_MODEL, IN_W)), _resident((1, QKV_W)),
            _resident((1, QKV_W)), _resident((NORM_W, NORM_W))]


def _inproj_prompt(x, consts, tm):
    B, S, _ = x.shape
    nt = S // tm
    assert S % tm == 0 and S >= max(WINDOWS) and tm >= HALO
    assert all(tm % (2 * SUBLANES * d) == 0 for d in DILATIONS), "a residue's rows must fill bf16 tiles"
    row = lambda width: pl.BlockSpec((tm, width), lambda b, i: (b * nt + i, 0))

    def group_spec(d):
        if d == 1:
            return row(ATTN_W)
        return pl.BlockSpec((None, d, tm // d, ATTN_W), lambda b, i: (b, 0, i, 0))

    def group_shape(d):
        return jax.ShapeDtypeStruct((B * S, ATTN_W) if d == 1 else (B, d, S // d, ATTN_W), BF)

    def tail_spec(w):
        if w >= tm:
            assert w % tm == 0
            first = (S - w) // tm
            return pl.BlockSpec((None, tm, ATTN_W), lambda b, i: (b, jnp.maximum(i - first, 0), 0))
        assert tm % w == 0
        return pl.BlockSpec((None, w, ATTN_W), lambda b, i: (b, 0, 0))

    groups = [group_spec(d) for d in DILATIONS]
    group_shapes = [group_shape(d) for d in DILATIONS]
    tails = [tail_spec(w) for w in WINDOWS]
    tail_shapes = [jax.ShapeDtypeStruct((B, w, ATTN_W), F32) for w in WINDOWS]
    n_stage = 2 * (PCH // LANES)
    M = B * S
    return pl.pallas_call(
        functools.partial(_inproj_kernel, True),
        grid=(B, nt),
        in_specs=[row(D_MODEL)] + _const_specs(),
        out_specs=groups * 3 + [row(CONV_CH), row(2 * D_MODEL)] + tails + tails
                  + [pl.BlockSpec((None, HALO, CONV_CH), lambda b, i: (b, 0, 0))],
        out_shape=group_shapes * 3
                  + [jax.ShapeDtypeStruct((M, CONV_CH), BF), jax.ShapeDtypeStruct((M, 2 * D_MODEL), BF)]
                  + tail_shapes + tail_shapes + [jax.ShapeDtypeStruct((B, HALO, CONV_CH), F32)],
        scratch_shapes=[pltpu.VMEM((n_stage, tm, LANES), F32)],
        compiler_params=_params(2),
        name="inproj_prompt",
    )(x.reshape(M, D_MODEL), *consts)


def _inproj_sample(x2d, consts):
    M = x2d.shape[0]
    full = lambda width: pl.BlockSpec((M, width), lambda i: (0, 0))
    return pl.pallas_call(
        functools.partial(_inproj_kernel, False),
        grid=(1,),
        in_specs=[full(D_MODEL)] + _const_specs(),
        out_specs=[full(QKV_W), full(QKV_W), full(QKV_W), full(CONV_CH), full(2 * D_MODEL)],
        out_shape=[jax.ShapeDtypeStruct((M, QKV_W), F32)] * 3
                  + [jax.ShapeDtypeStruct((M, CONV_CH), F32), jax.ShapeDtypeStruct((M, 2 * D_MODEL), BF)],
        compiler_params=_params(1),
        name="inproj_sample",
    )(x2d, *consts)


def _attn_kernel(q_ref, kc_ref, kp_ref, vc_ref, vp_ref, o_ref, lse_ref, kx_ref, vx_ref):
    for ri in range(q_ref.shape[0]):
        _attn_residue(q_ref.at[ri], kc_ref.at[ri], kp_ref.at[ri], vc_ref.at[ri], vp_ref.at[ri],
                      o_ref.at[ri], lse_ref.at[ri], kx_ref, vx_ref)


def _attn_residue(q_ref, kc_ref, kp_ref, vc_ref, vp_ref, o_ref, lse_ref, kx_ref, vx_ref):
    tq = q_ref.shape[0]
    not_first = pl.program_id(2) > 0
    kt = jnp.concatenate([kp_ref[...].T, kc_ref[...].T], axis=1)
    zero = jnp.zeros((HEAD_DIM, tq + BAND), BF)
    for h in range(N_HEADS):
        kh = kt[h * HEAD_DIM:(h + 1) * HEAD_DIM]
        lo, hi = (kh, zero) if h % 2 == 0 else (zero, kh)
        kx_ref[h * PAIR:h * PAIR + HEAD_DIM, :] = lo
        kx_ref[h * PAIR + HEAD_DIM:(h + 1) * PAIR, :] = hi
    ones = jnp.ones((tq + BAND, PAIR), BF)
    for pr in range(N_HEADS // 2):
        vx_ref[0:BAND, 2 * pr * PAIR:(2 * pr + 1) * PAIR] = vp_ref[:, pr * PAIR:(pr + 1) * PAIR]
        vx_ref[BAND:, 2 * pr * PAIR:(2 * pr + 1) * PAIR] = vc_ref[:, pr * PAIR:(pr + 1) * PAIR]
        vx_ref[:, (2 * pr + 1) * PAIR:(2 * pr + 2) * PAIR] = ones
    a = lax.broadcasted_iota(jnp.int32, (BAND, 2 * BAND), 0)
    c = lax.broadcasted_iota(jnp.int32, (BAND, 2 * BAND), 1)
    band = jnp.logical_and(c >= a, c <= a + BAND)
    band0 = jnp.logical_and(band, jnp.logical_or(c >= BAND, not_first))
    lane = lax.broadcasted_iota(jnp.int32, (BAND, LANES), 1)
    for sb in range(tq // BAND):
        rows = slice(sb * BAND, (sb + 1) * BAND)
        keys = slice(sb * BAND, (sb + 2) * BAND)
        mask = band0 if sb == 0 else band
        q = q_ref[rows, :]
        m_t = jnp.zeros((BAND, LANES), F32)
        l_t = jnp.ones((BAND, LANES), F32)
        for pr in range(N_HEADS // 2):
            vv = vx_ref[keys, 2 * pr * PAIR:(2 * pr + 2) * PAIR]
            pair = None
            for hh in range(2):
                h = 2 * pr + hh
                s = jnp.dot(q[:, pr * PAIR:(pr + 1) * PAIR], kx_ref[h * PAIR:(h + 1) * PAIR, keys],
                            preferred_element_type=F32)
                s = jnp.where(mask, s, NEG)
                m = jnp.max(s, axis=-1, keepdims=True)
                p = jnp.exp2(s - m)
                r = jnp.dot(p.astype(BF), vv, preferred_element_type=F32)
                l = r[:, PAIR:]
                contrib = r[:, :PAIR] / l
                pair = contrib if hh == 0 else jnp.where(lane < HEAD_DIM, pair, contrib)
                m_t = jnp.where(lane == h, m, m_t)
                l_t = jnp.where(lane == h, l, l_t)
            o_ref[rows, pr * PAIR:(pr + 1) * PAIR] = pair.astype(o_ref.dtype)
        lse_ref[rows, :] = m_t * LN2 + jnp.log(l_t)


def _attention_prompt(q, k, v, g):
    B, d, L, _ = q.shape
    assert WINDOWS[g] // d == BAND and L % BAND == 0
    tq = min(512, L)
    assert L % tq == 0
    sub = tq // BAND
    rb = min(d, 512 // tq)
    cur = pl.BlockSpec((None, rb, tq, ATTN_W), lambda b, r, j: (b, r, j, 0))
    prev = pl.BlockSpec((None, rb, BAND, ATTN_W), lambda b, r, j: (b, r, jnp.maximum(j * sub - 1, 0), 0))
    return pl.pallas_call(
        _attn_kernel,
        grid=(B, d // rb, L // tq),
        in_specs=[cur, cur, prev, cur, prev],
        out_specs=[cur, pl.BlockSpec((None, rb, tq, LANES), lambda b, r, j: (b, r, j, 0))],
        out_shape=[jax.ShapeDtypeStruct((B, d, L, ATTN_W), BF), jax.ShapeDtypeStruct((B, d, L, LANES), F32)],
        scratch_shapes=[pltpu.VMEM((N_HEADS * PAIR, tq + BAND), BF), pltpu.VMEM((tq + BAND, 2 * ATTN_W), BF)],
        compiler_params=_params(3),
        name="attn_g%d" % (g + 1),
    )(q, k, k, v, v)


def _ln_swish(c, g, b):
    mu = jnp.mean(c, axis=-1, keepdims=True)
    cc = c - mu
    var = jnp.mean(cc * cc, axis=-1, keepdims=True)
    y = cc * lax.rsqrt(var + EPS) * g + b
    return y * jax.nn.sigmoid(y)


CONV_ROWS = 64


def _conv_tile(has_context, u_ref, halo_ref, w_ref, b_ref, o_ref, ext_ref):
    tc = u_ref.shape[0]
    rc = CONV_ROWS
    n = tc + HALO
    ext_ref[0, 0:HALO, :] = jnp.where(has_context, halo_ref[...].astype(F32), 0.0)
    ext_ref[0, HALO:n, :] = u_ref[...].astype(F32)
    for s in range(1, SUBLANES):
        ext_ref[s, 0:n - SUBLANES, :] = ext_ref[0, s:n - SUBLANES + s, :]
    lead = HALO - (CONV_TAPS - 1)

    def row_chunk(i, carry):
        r0 = pl.multiple_of(i * rc, rc)
        for l0 in range(0, CONV_CH, LANES):
            ls = slice(l0, l0 + LANES)
            acc = jnp.broadcast_to(b_ref[:, ls], (rc, LANES))
            taps = w_ref[:, ls]
            for s in range(SUBLANES):
                js = [j for j in range(CONV_TAPS) if (lead + j) % SUBLANES == s]
                a_lo, a_hi = (lead + js[0]) // SUBLANES, (lead + js[-1]) // SUBLANES
                xs = ext_ref[s, pl.ds(r0 + a_lo * SUBLANES, rc + (a_hi - a_lo) * SUBLANES), ls]
                for j in js:
                    o8 = ((lead + j) // SUBLANES - a_lo) * SUBLANES
                    acc = acc + taps[j:j + 1, :] * xs[o8:o8 + rc]
            o_ref[pl.ds(r0, rc), ls] = acc.astype(o_ref.dtype)
        return carry

    lax.fori_loop(0, tc // rc, row_chunk, 0)


def _conv_kernel(*refs):
    _conv_tile(pl.program_id(1) > 0, *refs)


def _conv_specs(tc, step):
    hb = tc // HALO
    in_specs = [pl.BlockSpec((tc, CONV_CH), lambda *ids: (step(*ids), 0)),
                pl.BlockSpec((HALO, CONV_CH), lambda *ids: (jnp.maximum(step(*ids) * hb - 1, 0), 0)),
                _resident((CONV_TAPS, CONV_CH)), _resident((1, CONV_CH))]
    out_spec = pl.BlockSpec((tc, CONV_CH), lambda *ids: (step(*ids), 0))
    scratch = [pltpu.VMEM((SUBLANES, tc + HALO, CONV_CH), F32)]
    return in_specs, out_spec, scratch


def _conv_prompt(u, B, S, conv_w, conv_b, tc):
    nt = S // tc
    assert S % tc == 0 and tc % HALO == 0
    in_specs, out_spec, scratch = _conv_specs(tc, lambda b, i: b * nt + i)
    return pl.pallas_call(
        _conv_kernel,
        grid=(B, nt),
        in_specs=in_specs,
        out_specs=out_spec,
        out_shape=jax.ShapeDtypeStruct((B * S, CONV_CH), BF),
        scratch_shapes=scratch,
        compiler_params=_params(2),
        name="conv_prompt",
    )(u, u, conv_w, conv_b)


def _sample_conv_kernel(st_ref, un_ref, w_ref, b_ref, ca_ref, so_ref):
    n = CONV_TAPS - 1
    un = un_ref[...]
    acc = b_ref[...] + w_ref[n:n + 1, :] * un
    for j in range(n):
        acc = acc + w_ref[j:j + 1, :] * st_ref[j]
    ca_ref[...] = acc.astype(ca_ref.dtype)
    so_ref[0:n - 1] = st_ref[1:n]
    so_ref[n - 1] = un


def _conv_sample(state, u_new, conv_w, conv_b, bb):
    n, nb, _ = state.shape
    assert n == CONV_TAPS - 1 and nb % bb == 0
    return pl.pallas_call(
        _sample_conv_kernel,
        grid=(nb // bb,),
        in_specs=[pl.BlockSpec((n, bb, CONV_CH), lambda i: (0, i, 0)),
                  pl.BlockSpec((bb, CONV_CH), lambda i: (i, 0)),
                  _resident((CONV_TAPS, CONV_CH)), _resident((1, CONV_CH))],
        out_specs=[pl.BlockSpec((bb, CONV_CH), lambda i: (i, 0)),
                   pl.BlockSpec((n, bb, CONV_CH), lambda i: (0, i, 0))],
        out_shape=[jax.ShapeDtypeStruct((nb, CONV_CH), BF), jax.ShapeDtypeStruct(state.shape, F32)],
        compiler_params=_params(1),
        name="conv_sample",
    )(state, u_new, conv_w, conv_b)


def _sample_attn_kernel(knr_ref, knt_ref, vnt_ref, qbd_ref, k1, v1, k2, v2, k3, v3,
                        o_ref, ok1, ov1, ok2, ov2, ok3, ov3, bc_ref):
    for i, src in enumerate((knt_ref, vnt_ref)):
        for cl in range(N_GROUPS * HEAD_CHUNK):
            bc_ref[i, cl] = jnp.broadcast_to(src[:, cl:cl + 1], (HEAD_DIM, LANES))
    kin, vin = (k1, k2, k3), (v1, v2, v3)
    head_rows = [slice(hh * HEAD_DIM, (hh + 1) * HEAD_DIM) for hh in range(HEAD_CHUNK)]
    scores, new_scores, m = [], [], None
    for g in range(N_GROUPS):
        lb = kin[g].shape[1]
        s = jnp.dot(qbd_ref[g], kin[g][...].astype(BF), preferred_element_type=F32)
        sn = jnp.sum(qbd_ref[g].astype(F32) * knr_ref[g], axis=1, keepdims=True)
        if DILATIONS[g] > 1:
            pos = lax.broadcasted_iota(jnp.int32, (HEAD_CHUNK, lb), 1)
            s = jnp.where((pos & (DILATIONS[g] - 1)) == 0, s, NEG)
        mg = jnp.maximum(jnp.max(s, axis=1, keepdims=True), sn)
        m = mg if m is None else jnp.maximum(m, mg)
        scores.append(s)
        new_scores.append(sn)
    probs = [jnp.exp2(s - m) for s in scores]
    new_probs = [jnp.exp2(sn - m) for sn in new_scores]
    l = None
    for g in range(N_GROUPS):
        t = jnp.sum(probs[g], axis=1, keepdims=True) + new_probs[g]
        l = t if l is None else l + t
    inv = 1.0 / l
    nt = (((1,), (1,)), ((), ()))
    pv = [lax.dot_general(vin[g][...].astype(BF), probs[g].astype(BF), nt, preferred_element_type=F32)
          for g in range(N_GROUPS)]
    o_tile = jnp.zeros((HEAD_DIM, HEAD_CHUNK), F32)
    o_lane = lax.broadcasted_iota(jnp.int32, (HEAD_DIM, HEAD_CHUNK), 1)
    for hh in range(HEAD_CHUNK):
        acc = None
        for g in range(N_GROUPS):
            cl = g * HEAD_CHUNK + hh
            t = pv[g][head_rows[hh], :] + bc_ref[1, cl][:, 0:1] * new_probs[g][hh:hh + 1, :]
            acc = t if acc is None else acc + t
        o_tile = jnp.where(o_lane == hh, acc * inv[hh:hh + 1, :], o_tile)
    o_ref[...] = o_tile
    _shift_heads(kin, vin, (ok1, ok2, ok3), (ov1, ov2, ov3), bc_ref)


def _shift_heads(kin, vin, kout, vout, bc_ref):
    last = lax.broadcasted_iota(jnp.int32, (HEAD_DIM, LANES), 1) == LANES - 1

    def shift_in(dst, src, rows, new_col):
        lb = src.shape[1]
        rolled = pltpu.roll(src[rows, :], lb - 1, 1)
        if lb > LANES:
            dst[rows, 0:lb - LANES] = rolled[:, 0:lb - LANES]
        dst[rows, lb - LANES:lb] = jnp.where(last, new_col, rolled[:, lb - LANES:lb])

    for g in range(N_GROUPS):
        for head in range(HEAD_CHUNK):
            rows = slice(head * HEAD_DIM, (head + 1) * HEAD_DIM)
            cl = g * HEAD_CHUNK + head
            shift_in(kout[g], kin[g], rows, bc_ref[0, cl])
            shift_in(vout[g], vin[g], rows, bc_ref[1, cl])


def _columns(t):
    n = t.shape[0]
    t = t.reshape(n, N_GROUPS, N_HEADS // HEAD_CHUNK, HEAD_CHUNK, HEAD_DIM)
    return t.transpose(0, 2, 4, 1, 3).reshape(n, N_HEADS // HEAD_CHUNK, HEAD_DIM, N_GROUPS * HEAD_CHUNK)


def _head_rows(q):
    n = q.shape[0]
    nch = N_HEADS // HEAD_CHUNK
    t = q.reshape(n, N_GROUPS, nch, HEAD_CHUNK, HEAD_DIM).transpose(0, 2, 1, 3, 4)
    eye = jnp.eye(HEAD_CHUNK, dtype=q.dtype)
    t = t[:, :, :, :, None, :] * eye[:, :, None]
    return t.reshape(n, nch, N_GROUPS, HEAD_CHUNK, HEAD_CHUNK * HEAD_DIM).astype(BF)


N_SAMPLE_IN, N_SAMPLE_OUT, N_CONV_IN = 10, 7, 4


def _sample_attn_conv_kernel(tiles_per_seq, *refs):
    a, b, c = N_SAMPLE_IN, N_SAMPLE_IN + N_CONV_IN, N_SAMPLE_IN + N_CONV_IN + N_SAMPLE_OUT
    bc_ref, ext_ref = refs[c + 1:]
    _sample_attn_kernel(*refs[0:a], *refs[b:c], bc_ref)
    step = pl.program_id(0) * pl.num_programs(1) + pl.program_id(1)
    _conv_tile(step % tiles_per_seq > 0, *refs[a:b], refs[c], ext_ref)


def _attention_sample(q, k_new, v_new, caches, conv=None):
    nb = q.shape[0]
    nch = N_HEADS // HEAD_CHUNK
    rows = HEAD_CHUNK * HEAD_DIM
    for g, c in enumerate(caches):
        assert c.shape == (nb, ATTN_W, WINDOWS[g // 2]), "window buffers are expected full"
    col_spec = pl.BlockSpec((None, None, HEAD_DIM, N_GROUPS * HEAD_CHUNK), lambda b, h: (b, h, 0, 0))
    buf_specs = [pl.BlockSpec((None, rows, c.shape[2]), lambda b, h: (b, h, 0)) for c in caches]
    qbd_spec = pl.BlockSpec((None, None, N_GROUPS, HEAD_CHUNK, rows), lambda b, h: (b, h, 0, 0, 0))
    row_spec = pl.BlockSpec((None, None, N_GROUPS, 1, rows), lambda b, h: (b, h, 0, 0, 0))
    key_rows = k_new.reshape(nb, N_GROUPS, nch, 1, rows).transpose(0, 2, 1, 3, 4)
    in_specs = [row_spec] + [col_spec] * 2 + [qbd_spec] + buf_specs
    out_specs = [pl.BlockSpec((None, None, HEAD_DIM, HEAD_CHUNK), lambda b, h: (b, h, 0, 0))] + buf_specs
    out_shape = ([jax.ShapeDtypeStruct((nb, nch, HEAD_DIM, HEAD_CHUNK), F32)]
                 + [jax.ShapeDtypeStruct(c.shape, F32) for c in caches])
    scratch = [pltpu.VMEM((2, N_GROUPS * HEAD_CHUNK, HEAD_DIM, LANES), F32)]
    args = [key_rows, _columns(k_new), _columns(v_new), _head_rows(q), *caches]
    body, name = _sample_attn_kernel, "sample_attn"
    fuse = False
    if conv is not None:
        u, S = conv[0], conv[1]
        tc, rem = divmod(u.shape[0], nb * nch)
        fuse = rem == 0 and tc > 0 and S % tc == 0 and tc % 64 == 0
    if fuse:
        c_in, c_out, c_scratch = _conv_specs(tc, lambda b, h: b * nch + h)
        in_specs, out_specs, scratch = in_specs + c_in, out_specs + [c_out], scratch + c_scratch
        out_shape = out_shape + [jax.ShapeDtypeStruct(u.shape, BF)]
        args = args + [u, u, *conv[2:]]
        body, name = functools.partial(_sample_attn_conv_kernel, S // tc), "sample_attn_conv"
    outs = pl.pallas_call(
        body,
        grid=(nb, nch),
        in_specs=in_specs,
        out_specs=out_specs,
        out_shape=out_shape,
        scratch_shapes=scratch,
        compiler_params=_params(2),
        name=name,
    )(*args)
    o = outs[0].transpose(0, 1, 3, 2).reshape(nb, ATTN_W)
    return o, outs[1:1 + len(caches)], (outs[-1] if fuse else None)


def _main_kernel(combine, x_ref, *refs):
    if combine:
        o_refs, l_refs, e_ref = refs[0:3], refs[3:6], refs[6]
        refs = refs[7:]
        (cv_ref, sg_ref, lg_ref, lb_ref, wa_ref, wc_ref, wo_ref, g2_ref, w1_ref, w2_ref,
         y_ref, hid_ref, il_ref) = refs
    else:
        o_ref = refs[0]
        (cv_ref, sg_ref, lg_ref, lb_ref, wa_ref, wc_ref, wo_ref, g2_ref, w1_ref, w2_ref,
         y_ref, hid_ref) = refs[1:]
    tm = x_ref.shape[0]
    if combine:
        slot = [0]

        def token_major(ref, g):
            d = DILATIONS[g]
            if d == 1:
                return ref[...].astype(F32)
            nch = ref.shape[-1] // LANES
            base = slot[0]
            slot[0] += nch
            for r in range(d):
                t = ref[r].astype(F32)
                for c in range(nch):
                    il_ref[base + c, pl.ds(r, tm // d, stride=d), :] = t[:, c * LANES:(c + 1) * LANES]
            return jnp.concatenate([il_ref[base + c] for c in range(nch)], axis=-1)

        ls = [token_major(l_refs[g], g) for g in range(N_GROUPS)]
        m = jnp.maximum(jnp.maximum(ls[0], ls[1]), ls[2])
        es = [jnp.exp(t - m) for t in ls]
        inv = 1.0 / (es[0] + es[1] + es[2])
        o = None
        for g in range(N_GROUPS):
            alpha = es[g] * inv
            hi = alpha.astype(BF)
            lo = (alpha - hi.astype(F32)).astype(BF)
            wide = jnp.dot(jnp.concatenate([hi, lo], axis=-1), e_ref[...], preferred_element_type=F32)
            t = wide * token_major(o_refs[g], g)
            o = t if o is None else o + t
    else:
        o = o_ref[...]
    a = jnp.dot(o.astype(BF), wa_ref[...], preferred_element_type=F32)
    ca = _ln_swish(cv_ref[...].astype(F32), lg_ref[...], lb_ref[...]).astype(BF)
    c = jnp.dot(ca, wc_ref[...], preferred_element_type=F32)
    merged = sg_ref[:, :D_MODEL].astype(F32) * a + sg_ref[:, D_MODEL:].astype(F32) * c
    h = x_ref[...] + jnp.dot(merged.astype(BF), wo_ref[...], preferred_element_type=F32)
    hn = (h * lax.rsqrt(jnp.mean(h * h, axis=-1, keepdims=True) + EPS) * g2_ref[...]).astype(BF)
    fch = 512
    for c0 in range(0, D_FF, fch):
        t = jnp.maximum(jnp.dot(hn, w1_ref[:, c0:c0 + fch], preferred_element_type=F32), 0.0)
        hid_ref[:, c0:c0 + fch] = (t * t).astype(BF)
    y_ref[...] = h + jnp.dot(hid_ref[...], w2_ref[...], preferred_element_type=F32)


def _main(x2d, attn, ca, sg, weights, tm, nt=1):
    M = x2d.shape[0]
    assert M % (tm * nt) == 0
    B = M // (tm * nt)
    row = lambda width: pl.BlockSpec((tm, width), lambda b, i: (b * nt + i, 0))
    combine = isinstance(attn, tuple)
    scratch = [pltpu.VMEM((tm, D_FF), BF)]
    if combine:
        os_, ls_ = attn

        def group_spec(d, width):
            if d == 1:
                return row(width)
            return pl.BlockSpec((None, d, tm // d, width), lambda b, i: (b, 0, i, 0))

        head = jnp.arange(ATTN_W) // HEAD_DIM
        spread = (jnp.arange(LANES)[:, None] == head[None, :]).astype(BF)
        attn_args = list(os_) + list(ls_) + [jnp.concatenate([spread, spread], axis=0)]
        attn_specs = ([group_spec(d, ATTN_W) for d in DILATIONS] + [group_spec(d, LANES) for d in DILATIONS]
                      + [_resident((2 * LANES, ATTN_W))])
        n_il = sum((ATTN_W + LANES) // LANES for d in DILATIONS if d > 1)
        scratch.append(pltpu.VMEM((n_il, tm, LANES), F32))
    else:
        attn_args, attn_specs = [attn], [row(ATTN_W)]
    return pl.pallas_call(
        functools.partial(_main_kernel, combine),
        grid=(B, nt),
        in_specs=[row(D_MODEL)] + attn_specs + [row(CONV_CH), row(2 * D_MODEL)]
                 + [_resident(w.shape) for w in weights],
        out_specs=row(D_MODEL),
        out_shape=jax.ShapeDtypeStruct((M, D_MODEL), F32),
        scratch_shapes=scratch,
        compiler_params=_params(2),
        name="main_prompt" if combine else "main_sample",
    )(x2d, *attn_args, ca, sg, *weights)


def kernel(x_prompt, x_sample, cache_k1, cache_v1, cache_k2, cache_v2, cache_k3, cache_v3, state_conv,
           norm1_g, w_in, q_norm_g, k_norm_g, conv_w, conv_b, conv_ln_g, conv_ln_b,
           w_attn_out, w_conv_out, w_o, norm2_g, w_ff1, w_ff2):
    assert w_in.shape[0] == 1, "one layer"
    B, S, _ = x_prompt.shape
    nb, ns, _ = x_sample.shape
    assert ns == 1, "one new token per sample row"
    tm = 512
    consts = _inproj_consts(norm1_g[0], w_in[0], q_norm_g[0], k_norm_g[0])
    conv_args = (conv_w[0], conv_b)
    weights = (conv_ln_g, conv_ln_b, w_attn_out[0].astype(BF), w_conv_out[0].astype(BF), w_o[0].astype(BF),
               norm2_g, w_ff1[0].astype(BF), w_ff2[0].astype(BF))

    outs = _inproj_prompt(x_prompt, consts, tm)
    qs, ks, vs = outs[0:3], outs[3:6], outs[6:9]
    u, sg = outs[9:11]
    kts, vts, ut = outs[11:14], outs[14:17], outs[17]
    attn_o, attn_l = [], []
    for g, d in enumerate(DILATIONS):
        as4 = lambda t: t.reshape(B, d, S // d, ATTN_W)
        o, lse = _attention_prompt(as4(qs[g]), as4(ks[g]), as4(vs[g]), g)
        attn_o.append(o.reshape(B * S, ATTN_W) if d == 1 else o)
        attn_l.append(lse.reshape(B * S, LANES) if d == 1 else lse)
    tail = lambda t: t.reshape(1, B, t.shape[1], N_HEADS, HEAD_DIM)
    p_conv = ut[None, :, HALO - (CONV_TAPS - 1):, :]

    q_s, k_s, v_s, u_s, sg_s = _inproj_sample(x_sample.reshape(nb, D_MODEL), consts)
    to_buf = lambda c: c[0].transpose(0, 2, 3, 1).reshape(nb, ATTN_W, c.shape[2])
    from_buf = lambda c: c.reshape(nb, N_HEADS, HEAD_DIM, c.shape[2]).transpose(0, 3, 1, 2)[None]
    o_s, bufs, ca = _attention_sample(q_s, k_s, v_s, [to_buf(c) for c in
                                                      (cache_k1, cache_v1, cache_k2, cache_v2, cache_k3, cache_v3)],
                                      conv=(u, S, *conv_args))
    if ca is None:
        ca = _conv_prompt(u, B, S, *conv_args, tc=tm)
    y_prompt = _main(x_prompt.reshape(B * S, D_MODEL), (tuple(attn_o), tuple(attn_l)), ca, sg, weights,
                     tm=tm, nt=S // tm).reshape(B, S, D_MODEL)
    ca_s, state_new = _conv_sample(state_conv[0].transpose(1, 0, 2), u_s, *conv_args, bb=min(32, nb))
    y_sample = _main(x_sample.reshape(nb, D_MODEL), o_s, ca_s, sg_s, weights, tm=nb).reshape(nb, 1, D_MODEL)
    s_conv = state_new.transpose(1, 0, 2)[None]

    kv_tails = [tail(t) for pair in zip(kts, vts) for t in pair]
    return (y_prompt, y_sample, *kv_tails, p_conv, *(from_buf(b) for b in bufs), s_conv)
```

```python
import functools
import math

import jax
import jax.numpy as jnp
from jax import lax
from jax.experimental import pallas as pl
from jax.experimental.pallas import tpu as pltpu

F32, BF = jnp.float32, jnp.bfloat16

D_MODEL = 1024
N_GROUPS = 3
WINDOWS = (128, 512, 2048)
DILATIONS = (1, 4, 16)
N_HEADS = 8
HEAD_DIM = 64
ATTN_W = N_HEADS * HEAD_DIM
QKV_W = N_GROUPS * ATTN_W
CONV_CH = D_MODEL
CONV_TAPS = 31
D_FF = 4 * D_MODEL
IN_W = 3 * QKV_W + 2 * CONV_CH + 2 * D_MODEL
EPS = 1e-6
LOG2E = math.log2(math.e)
LN2 = math.log(2.0)
Q_SCALE = LOG2E / 8.0
BAND = 128
NEG = -1e30

LANES = 128
SUBLANES = 8
PAIR = 2 * HEAD_DIM
HALO = 32
PCH = 512
NORM_W = 256
HEAD_CHUNK = 8
ATTN_STEP = 1024
VMEM_LIMIT = 60 << 20


def _params(n_axes, vmem=VMEM_LIMIT):
    return pltpu.CompilerParams(dimension_semantics=("arbitrary",) * n_axes, vmem_limit_bytes=vmem)


def _resident(shape):
    nd = len(shape)
    return pl.BlockSpec(shape, lambda *_: (0,) * nd, pipeline_mode=pl.Buffered(1))


def _inproj_kernel(prompt, x_ref, g1_ref, w_ref, gq_ref, gk_ref, p_ref, *refs):
    if prompt:
        qs, ks, vs = refs[0:3], refs[3:6], refs[6:9]
        u_ref, sg_ref = refs[9:11]
        kts, vts = refs[11:14], refs[14:17]
        ut_ref, stage_ref = refs[17:19]
    else:
        q_ref, k_ref, v_ref, u_ref, sg_ref = refs
    tm = x_ref.shape[0]
    x = x_ref[...]
    xn = (x * lax.rsqrt(jnp.mean(x * x, axis=-1, keepdims=True) + EPS) * g1_ref[...]).astype(BF)

    def proj(c0):
        return jnp.dot(xn, w_ref[:, c0:c0 + PCH], preferred_element_type=F32)

    def head_norm(t, gain):
        parts = []
        for c0 in range(0, PCH, NORM_W):
            th = t[:, c0:c0 + NORM_W]
            ms = jnp.dot((th * th).astype(BF), p_ref[...], preferred_element_type=F32)
            parts.append(th * lax.rsqrt(ms + EPS) * gain[:, c0:c0 + NORM_W])
        return jnp.concatenate(parts, axis=-1)

    slot = [0]

    def put(dst, g, off, t):
        d = DILATIONS[g]
        if d == 1:
            dst[:, off:off + PCH] = t.astype(dst.dtype)
            return
        base = slot[0] % stage_ref.shape[0]
        slot[0] += PCH // LANES
        for c in range(PCH // LANES):
            stage_ref[base + c] = t[:, c * LANES:(c + 1) * LANES]
        for r in range(d):
            parts = [stage_ref[base + c, pl.ds(r, tm // d, stride=d), :] for c in range(PCH // LANES)]
            dst[r, :, off:off + PCH] = jnp.concatenate(parts, axis=-1).astype(dst.dtype)

    for c in range(QKV_W // PCH):
        cs = slice(c * PCH, (c + 1) * PCH)
        g, off = divmod(c * PCH, ATTN_W)
        qn = head_norm(proj(c * PCH), gq_ref[:, cs])
        kn = head_norm(proj(QKV_W + c * PCH), gk_ref[:, cs])
        vv = proj(2 * QKV_W + c * PCH)
        if prompt:
            put(qs[g], g, off, qn)
            put(ks[g], g, off, kn)
            put(vs[g], g, off, vv)
            rows = kts[g].shape[0]
            kts[g][:, off:off + PCH] = kn[tm - rows:, :]
            vts[g][:, off:off + PCH] = vv[tm - rows:, :]
        else:
            q_ref[:, cs] = qn
            k_ref[:, cs] = kn
            v_ref[:, cs] = vv
    for c in range(CONV_CH // PCH):
        cs = slice(c * PCH, (c + 1) * PCH)
        u = proj(3 * QKV_W + c * PCH) * jax.nn.sigmoid(proj(3 * QKV_W + CONV_CH + c * PCH))
        u_ref[:, cs] = u.astype(u_ref.dtype)
        if prompt:
            ut_ref[:, cs] = u[tm - HALO:, :]
    for c in range(2 * D_MODEL // PCH):
        cs = slice(c * PCH, (c + 1) * PCH)
        sg_ref[:, cs] = jax.nn.sigmoid(proj(3 * QKV_W + 2 * CONV_CH + c * PCH)).astype(sg_ref.dtype)


def _inproj_consts(norm1_g, w_in, q_norm_g, k_norm_g):
    gq = (jnp.repeat(q_norm_g, N_HEADS, axis=0) * Q_SCALE).reshape(1, QKV_W)
    gk = jnp.repeat(k_norm_g, N_HEADS, axis=0).reshape(1, QKV_W)
    head = jnp.arange(NORM_W) // HEAD_DIM
    pmat = jnp.where(head[:, None] == head[None, :], 1.0 / HEAD_DIM, 0.0).astype(BF)
    return norm1_g.reshape(1, D_MODEL), w_in.astype(BF), gq, gk, pmat


def _const_specs():
    return [_resident((1, D_MODEL)), _resident((D_MODEL, IN_W)), _resident((1, QKV_W)),
            _resident((1, QKV_W)), _resident((NORM_W, NORM_W))]


def _inproj_prompt(x, consts, tm):
    B, S, _ = x.shape
    nt = S // tm
    assert S % tm == 0 and S >= max(WINDOWS) and tm >= HALO
    assert all(tm % (2 * SUBLANES * d) == 0 for d in DILATIONS), "a residue's rows must fill bf16 tiles"
    row = lambda width: pl.BlockSpec((tm, width), lambda b, i: (b * nt + i, 0))

    def group_spec(d):
        if d == 1:
            return row(ATTN_W)
        return pl.BlockSpec((None, d, tm // d, ATTN_W), lambda b, i: (b, 0, i, 0))

    def group_shape(d):
        return jax.ShapeDtypeStruct((B * S, ATTN_W) if d == 1 else (B, d, S // d, ATTN_W), BF)

    def tail_spec(w):
        if w >= tm:
            assert w % tm == 0
            first = (S - w) // tm
            return pl.BlockSpec((None, tm, ATTN_W), lambda b, i: (b, jnp.maximum(i - first, 0), 0))
        assert tm % w == 0
        return pl.BlockSpec((None, w, ATTN_W), lambda b, i: (b, 0, 0))

    groups = [group_spec(d) for d in DILATIONS]
    group_shapes = [group_shape(d) for d in DILATIONS]
    tails = [tail_spec(w) for w in WINDOWS]
    tail_shapes = [jax.ShapeDtypeStruct((B, w, ATTN_W), F32) for w in WINDOWS]
    n_stage = 2 * (PCH // LANES)
    M = B * S
    return pl.pallas_call(
        functools.partial(_inproj_kernel, True),
        grid=(B, nt),
        in_specs=[row(D_MODEL)] + _const_specs(),
        out_specs=groups * 3 + [row(CONV_CH), row(2 * D_MODEL)] + tails + tails
                  + [pl.BlockSpec((None, HALO, CONV_CH), lambda b, i: (b, 0, 0))],
        out_shape=group_shapes * 3
                  + [jax.ShapeDtypeStruct((M, CONV_CH), BF), jax.ShapeDtypeStruct((M, 2 * D_MODEL), BF)]
                  + tail_shapes + tail_shapes + [jax.ShapeDtypeStruct((B, HALO, CONV_CH), F32)],
        scratch_shapes=[pltpu.VMEM((n_stage, tm, LANES), F32)],
        compiler_params=_params(2),
        name="inproj_prompt",
    )(x.reshape(M, D_MODEL), *consts)


def _inproj_sample(x2d, consts):
    M = x2d.shape[0]
    full = lambda width: pl.BlockSpec((M, width), lambda i: (0, 0))
    return pl.pallas_call(
        functools.partial(_inproj_kernel, False),
        grid=(1,),
        in_specs=[full(D_MODEL)] + _const_specs(),
        out_specs=[full(QKV_W), full(QKV_W), full(QKV_W), full(CONV_CH), full(2 * D_MODEL)],
        out_shape=[jax.ShapeDtypeStruct((M, QKV_W), F32)] * 3
                  + [jax.ShapeDtypeStruct((M, CONV_CH), F32), jax.ShapeDtypeStruct((M, 2 * D_MODEL), BF)],
        compiler_params=_params(1),
        name="inproj_sample",
    )(x2d, *consts)


def _attn_kernel(q_ref, kc_ref, kp_ref, vc_ref, vp_ref, o_ref, lse_ref, kx_ref, vx_ref):
    for ri in range(q_ref.shape[0]):
        _attn_residue(q_ref.at[ri], kc_ref.at[ri], kp_ref.at[ri], vc_ref.at[ri], vp_ref.at[ri],
                      o_ref.at[ri], lse_ref.at[ri], kx_ref, vx_ref)


def _attn_residue(q_ref, kc_ref, kp_ref, vc_ref, vp_ref, o_ref, lse_ref, kx_ref, vx_ref):
    tq = q_ref.shape[0]
    not_first = pl.program_id(2) > 0
    kt = jnp.concatenate([kp_ref[...].T, kc_ref[...].T], axis=1)
    zero = jnp.zeros((HEAD_DIM, tq + BAND), BF)
    for h in range(N_HEADS):
        kh = kt[h * HEAD_DIM:(h + 1) * HEAD_DIM]
        lo, hi = (kh, zero) if h % 2 == 0 else (zero, kh)
        kx_ref[h * PAIR:h * PAIR + HEAD_DIM, :] = lo
        kx_ref[h * PAIR + HEAD_DIM:(h + 1) * PAIR, :] = hi
    ones = jnp.ones((tq + BAND, PAIR), BF)
    for pr in range(N_HEADS // 2):
        vx_ref[0:BAND, 2 * pr * PAIR:(2 * pr + 1) * PAIR] = vp_ref[:, pr * PAIR:(pr + 1) * PAIR]
        vx_ref[BAND:, 2 * pr * PAIR:(2 * pr + 1) * PAIR] = vc_ref[:, pr * PAIR:(pr + 1) * PAIR]
        vx_ref[:, (2 * pr + 1) * PAIR:(2 * pr + 2) * PAIR] = ones
    a = lax.broadcasted_iota(jnp.int32, (BAND, 2 * BAND), 0)
    c = lax.broadcasted_iota(jnp.int32, (BAND, 2 * BAND), 1)
    band = jnp.logical_and(c >= a, c <= a + BAND)
    band0 = jnp.logical_and(band, jnp.logical_or(c >= BAND, not_first))
    lane = lax.broadcasted_iota(jnp.int32, (BAND, LANES), 1)
    for sb in range(tq // BAND):
        rows = slice(sb * BAND, (sb + 1) * BAND)
        keys = slice(sb * BAND, (sb + 2) * BAND)
        mask = band0 if sb == 0 else band
        q = q_ref[rows, :]
        m_t = jnp.zeros((BAND, LANES), F32)
        l_t = jnp.ones((BAND, LANES), F32)
        for pr in range(N_HEADS // 2):
            vv = vx_ref[keys, 2 * pr * PAIR:(2 * pr + 2) * PAIR]
            pair = None
            for hh in range(2):
                h = 2 * pr + hh
                s = jnp.dot(q[:, pr * PAIR:(pr + 1) * PAIR], kx_ref[h * PAIR:(h + 1) * PAIR, keys],
                            preferred_element_type=F32)
                s = jnp.where(mask, s, NEG)
                m = jnp.max(s, axis=-1, keepdims=True)
                p = jnp.exp2(s - m)
                r = jnp.dot(p.astype(BF), vv, preferred_element_type=F32)
                l = r[:, PAIR:]
                contrib = r[:, :PAIR] / l
                pair = contrib if hh == 0 else jnp.where(lane < HEAD_DIM, pair, contrib)
                m_t = jnp.where(lane == h, m, m_t)
                l_t = jnp.where(lane == h, l, l_t)
            o_ref[rows, pr * PAIR:(pr + 1) * PAIR] = pair.astype(o_ref.dtype)
        lse_ref[rows, :] = m_t * LN2 + jnp.log(l_t)


def _attention_prompt(q, k, v, g):
    B, d, L, _ = q.shape
    assert WINDOWS[g] // d == BAND and L % BAND == 0
    tq = min(ATTN_STEP, L)
    assert L % tq == 0
    sub = tq // BAND
    rb = min(d, ATTN_STEP // tq)
    cur = pl.BlockSpec((None, rb, tq, ATTN_W), lambda b, r, j: (b, r, j, 0))
    prev = pl.BlockSpec((None, rb, BAND, ATTN_W), lambda b, r, j: (b, r, jnp.maximum(j * sub - 1, 0), 0))
    return pl.pallas_call(
        _attn_kernel,
        grid=(B, d // rb, L // tq),
        in_specs=[cur, cur, prev, cur, prev],
        out_specs=[cur, pl.BlockSpec((None, rb, tq, LANES), lambda b, r, j: (b, r, j, 0))],
        out_shape=[jax.ShapeDtypeStruct((B, d, L, ATTN_W), BF), jax.ShapeDtypeStruct((B, d, L, LANES), F32)],
        scratch_shapes=[pltpu.VMEM((N_HEADS * PAIR, tq + BAND), BF), pltpu.VMEM((tq + BAND, 2 * ATTN_W), BF)],
        compiler_params=_params(3),
        name="attn_g%d" % (g + 1),
    )(q, k, k, v, v)


def _ln_swish(c, g, b):
    mu = jnp.mean(c, axis=-1, keepdims=True)
    cc = c - mu
    var = jnp.mean(cc * cc, axis=-1, keepdims=True)
    y = cc * lax.rsqrt(var + EPS) * g + b
    return y * jax.nn.sigmoid(y)


CONV_ROWS = 64


def _conv_tile(has_context, u_ref, halo_ref, w_ref, b_ref, o_ref, ext_ref):
    tc = u_ref.shape[0]
    rc = CONV_ROWS
    n = tc + HALO
    ext_ref[0, 0:HALO, :] = jnp.where(has_context, halo_ref[...].astype(F32), 0.0)
    ext_ref[0, HALO:n, :] = u_ref[...].astype(F32)
    for s in range(1, SUBLANES):
        ext_ref[s, 0:n - SUBLANES, :] = ext_ref[0, s:n - SUBLANES + s, :]
    lead = HALO - (CONV_TAPS - 1)

    def row_chunk(i, carry):
        r0 = pl.multiple_of(i * rc, rc)
        for l0 in range(0, CONV_CH, LANES):
            ls = slice(l0, l0 + LANES)
            acc = jnp.broadcast_to(b_ref[:, ls], (rc, LANES))
            taps = w_ref[:, ls]
            for s in range(SUBLANES):
                js = [j for j in range(CONV_TAPS) if (lead + j) % SUBLANES == s]
                a_lo, a_hi = (lead + js[0]) // SUBLANES, (lead + js[-1]) // SUBLANES
                xs = ext_ref[s, pl.ds(r0 + a_lo * SUBLANES, rc + (a_hi - a_lo) * SUBLANES), ls]
                for j in js:
                    o8 = ((lead + j) // SUBLANES - a_lo) * SUBLANES
                    acc = acc + taps[j:j + 1, :] * xs[o8:o8 + rc]
            o_ref[pl.ds(r0, rc), ls] = acc.astype(o_ref.dtype)
        return carry

    lax.fori_loop(0, tc // rc, row_chunk, 0)


def _conv_kernel(*refs):
    _conv_tile(pl.program_id(1) > 0, *refs)


def _conv_specs(tc, step):
    hb = tc // HALO
    in_specs = [pl.BlockSpec((tc, CONV_CH), lambda *ids: (step(*ids), 0)),
                pl.BlockSpec((HALO, CONV_CH), lambda *ids: (jnp.maximum(step(*ids) * hb - 1, 0), 0)),
                _resident((CONV_TAPS, CONV_CH)), _resident((1, CONV_CH))]
    out_spec = pl.BlockSpec((tc, CONV_CH), lambda *ids: (step(*ids), 0))
    scratch = [pltpu.VMEM((SUBLANES, tc + HALO, CONV_CH), F32)]
    return in_specs, out_spec, scratch


def _conv_prompt(u, B, S, conv_w, conv_b, tc):
    nt = S // tc
    assert S % tc == 0 and tc % HALO == 0
    in_specs, out_spec, scratch = _conv_specs(tc, lambda b, i: b * nt + i)
    return pl.pallas_call(
        _conv_kernel,
        grid=(B, nt),
        in_specs=in_specs,
        out_specs=out_spec,
        out_shape=jax.ShapeDtypeStruct((B * S, CONV_CH), BF),
        scratch_shapes=scratch,
        compiler_params=_params(2),
        name="conv_prompt",
    )(u, u, conv_w, conv_b)


def _sample_conv_kernel(st_ref, un_ref, w_ref, b_ref, ca_ref, so_ref):
    n = CONV_TAPS - 1
    un = un_ref[...]
    acc = b_ref[...] + w_ref[n:n + 1, :] * un
    for j in range(n):
        acc = acc + w_ref[j:j + 1, :] * st_ref[j]
    ca_ref[...] = acc.astype(ca_ref.dtype)
    so_ref[0:n - 1] = st_ref[1:n]
    so_ref[n - 1] = un


def _conv_sample(state, u_new, conv_w, conv_b, bb):
    n, nb, _ = state.shape
    assert n == CONV_TAPS - 1 and nb % bb == 0
    return pl.pallas_call(
        _sample_conv_kernel,
        grid=(nb // bb,),
        in_specs=[pl.BlockSpec((n, bb, CONV_CH), lambda i: (0, i, 0)),
                  pl.BlockSpec((bb, CONV_CH), lambda i: (i, 0)),
                  _resident((CONV_TAPS, CONV_CH)), _resident((1, CONV_CH))],
        out_specs=[pl.BlockSpec((bb, CONV_CH), lambda i: (i, 0)),
                   pl.BlockSpec((n, bb, CONV_CH), lambda i: (0, i, 0))],
        out_shape=[jax.ShapeDtypeStruct((nb, CONV_CH), BF), jax.ShapeDtypeStruct(state.shape, F32)],
        compiler_params=_params(1),
        name="conv_sample",
    )(state, u_new, conv_w, conv_b)


def _sample_attn_kernel(knr_ref, knt_ref, vnt_ref, qbd_ref, k1, v1, k2, v2, k3, v3,
                        o_ref, ok1, ov1, ok2, ov2, ok3, ov3, bc_ref):
    for i, src in enumerate((knt_ref, vnt_ref)):
        for cl in range(N_GROUPS * HEAD_CHUNK):
            bc_ref[i, cl] = jnp.broadcast_to(src[:, cl:cl + 1], (HEAD_DIM, LANES))
    kin, vin = (k1, k2, k3), (v1, v2, v3)
    head_rows = [slice(hh * HEAD_DIM, (hh + 1) * HEAD_DIM) for hh in range(HEAD_CHUNK)]
    scores, new_scores, m = [], [], None
    for g in range(N_GROUPS):
        lb = kin[g].shape[1]
        s = jnp.dot(qbd_ref[g], kin[g][...].astype(BF), preferred_element_type=F32)
        sn = jnp.sum(qbd_ref[g].astype(F32) * knr_ref[g], axis=1, keepdims=True)
        if DILATIONS[g] > 1:
            pos = lax.broadcasted_iota(jnp.int32, (HEAD_CHUNK, lb), 1)
            s = jnp.where((pos & (DILATIONS[g] - 1)) == 0, s, NEG)
        mg = jnp.maximum(jnp.max(s, axis=1, keepdims=True), sn)
        m = mg if m is None else jnp.maximum(m, mg)
        scores.append(s)
        new_scores.append(sn)
    probs = [jnp.exp2(s - m) for s in scores]
    new_probs = [jnp.exp2(sn - m) for sn in new_scores]
    l = None
    for g in range(N_GROUPS):
        t = jnp.sum(probs[g], axis=1, keepdims=True) + new_probs[g]
        l = t if l is None else l + t
    inv = 1.0 / l
    nt = (((1,), (1,)), ((), ()))
    pv = [lax.dot_general(vin[g][...].astype(BF), probs[g].astype(BF), nt, preferred_element_type=F32)
          for g in range(N_GROUPS)]
    o_tile = jnp.zeros((HEAD_DIM, HEAD_CHUNK), F32)
    o_lane = lax.broadcasted_iota(jnp.int32, (HEAD_DIM, HEAD_CHUNK), 1)
    for hh in range(HEAD_CHUNK):
        acc = None
        for g in range(N_GROUPS):
            cl = g * HEAD_CHUNK + hh
            t = pv[g][head_rows[hh], :] + bc_ref[1, cl][:, 0:1] * new_probs[g][hh:hh + 1, :]
            acc = t if acc is None else acc + t
        o_tile = jnp.where(o_lane == hh, acc * inv[hh:hh + 1, :], o_tile)
    o_ref[...] = o_tile
    _shift_heads(kin, vin, (ok1, ok2, ok3), (ov1, ov2, ov3), bc_ref)


def _shift_heads(kin, vin, kout, vout, bc_ref):
    last = lax.broadcasted_iota(jnp.int32, (HEAD_DIM, LANES), 1) == LANES - 1

    def shift_in(dst, src, rows, new_col):
        lb = src.shape[1]
        rolled = pltpu.roll(src[rows, :], lb - 1, 1)
        if lb > LANES:
            dst[rows, 0:lb - LANES] = rolled[:, 0:lb - LANES]
        dst[rows, lb - LANES:lb] = jnp.where(last, new_col, rolled[:, lb - LANES:lb])

    for g in range(N_GROUPS):
        for head in range(HEAD_CHUNK):
            rows = slice(head * HEAD_DIM, (head + 1) * HEAD_DIM)
            cl = g * HEAD_CHUNK + head
            shift_in(kout[g], kin[g], rows, bc_ref[0, cl])
            shift_in(vout[g], vin[g], rows, bc_ref[1, cl])


def _columns(t):
    n = t.shape[0]
    t = t.reshape(n, N_GROUPS, N_HEADS // HEAD_CHUNK, HEAD_CHUNK, HEAD_DIM)
    return t.transpose(0, 2, 4, 1, 3).reshape(n, N_HEADS // HEAD_CHUNK, HEAD_DIM, N_GROUPS * HEAD_CHUNK)


def _head_rows(q):
    n = q.shape[0]
    nch = N_HEADS // HEAD_CHUNK
    t = q.reshape(n, N_GROUPS, nch, HEAD_CHUNK, HEAD_DIM).transpose(0, 2, 1, 3, 4)
    eye = jnp.eye(HEAD_CHUNK, dtype=q.dtype)
    t = t[:, :, :, :, None, :] * eye[:, :, None]
    return t.reshape(n, nch, N_GROUPS, HEAD_CHUNK, HEAD_CHUNK * HEAD_DIM).astype(BF)


N_SAMPLE_IN, N_SAMPLE_OUT, N_CONV_IN = 10, 7, 4


def _sample_attn_conv_kernel(tiles_per_seq, *refs):
    a, b, c = N_SAMPLE_IN, N_SAMPLE_IN + N_CONV_IN, N_SAMPLE_IN + N_CONV_IN + N_SAMPLE_OUT
    bc_ref, ext_ref = refs[c + 1:]
    _sample_attn_kernel(*refs[0:a], *refs[b:c], bc_ref)
    step = pl.program_id(0) * pl.num_programs(1) + pl.program_id(1)
    _conv_tile(step % tiles_per_seq > 0, *refs[a:b], refs[c], ext_ref)


def _attention_sample(q, k_new, v_new, caches, conv=None):
    nb = q.shape[0]
    nch = N_HEADS // HEAD_CHUNK
    rows = HEAD_CHUNK * HEAD_DIM
    for g, c in enumerate(caches):
        assert c.shape == (nb, ATTN_W, WINDOWS[g // 2]), "window buffers are expected full"
    col_spec = pl.BlockSpec((None, None, HEAD_DIM, N_GROUPS * HEAD_CHUNK), lambda b, h: (b, h, 0, 0))
    buf_specs = [pl.BlockSpec((None, rows, c.shape[2]), lambda b, h: (b, h, 0)) for c in caches]
    qbd_spec = pl.BlockSpec((None, None, N_GROUPS, HEAD_CHUNK, rows), lambda b, h: (b, h, 0, 0, 0))
    row_spec = pl.BlockSpec((None, None, N_GROUPS, 1, rows), lambda b, h: (b, h, 0, 0, 0))
    key_rows = k_new.reshape(nb, N_GROUPS, nch, 1, rows).transpose(0, 2, 1, 3, 4)
    in_specs = [row_spec] + [col_spec] * 2 + [qbd_spec] + buf_specs
    out_specs = [pl.BlockSpec((None, None, HEAD_DIM, HEAD_CHUNK), lambda b, h: (b, h, 0, 0))] + buf_specs
    out_shape = ([jax.ShapeDtypeStruct((nb, nch, HEAD_DIM, HEAD_CHUNK), F32)]
                 + [jax.ShapeDtypeStruct(c.shape, F32) for c in caches])
    scratch = [pltpu.VMEM((2, N_GROUPS * HEAD_CHUNK, HEAD_DIM, LANES), F32)]
    args = [key_rows, _columns(k_new), _columns(v_new), _head_rows(q), *caches]
    body, name = _sample_attn_kernel, "sample_attn"
    fuse = False
    if conv is not None:
        u, S = conv[0], conv[1]
        tc, rem = divmod(u.shape[0], nb * nch)
        fuse = rem == 0 and tc > 0 and S % tc == 0 and tc % 64 == 0
    if fuse:
        c_in, c_out, c_scratch = _conv_specs(tc, lambda b, h: b * nch + h)
        in_specs, out_specs, scratch = in_specs + c_in, out_specs + [c_out], scratch + c_scratch
        out_shape = out_shape + [jax.ShapeDtypeStruct(u.shape, BF)]
        args = args + [u, u, *conv[2:]]
        body, name = functools.partial(_sample_attn_conv_kernel, S // tc), "sample_attn_conv"
    outs = pl.pallas_call(
        body,
        grid=(nb, nch),
        in_specs=in_specs,
        out_specs=out_specs,
        out_shape=out_shape,
        scratch_shapes=scratch,
        compiler_params=_params(2),
        name=name,
    )(*args)
    o = outs[0].transpose(0, 1, 3, 2).reshape(nb, ATTN_W)
    return o, outs[1:1 + len(caches)], (outs[-1] if fuse else None)


def _main_kernel(combine, x_ref, *refs):
    if combine:
        o_refs, l_refs, e_ref = refs[0:3], refs[3:6], refs[6]
        refs = refs[7:]
        (cv_ref, sg_ref, lg_ref, lb_ref, wa_ref, wc_ref, wo_ref, g2_ref, w1_ref, w2_ref,
         y_ref, hid_ref, il_ref) = refs
    else:
        o_ref = refs[0]
        (cv_ref, sg_ref, lg_ref, lb_ref, wa_ref, wc_ref, wo_ref, g2_ref, w1_ref, w2_ref,
         y_ref, hid_ref) = refs[1:]
    tm = x_ref.shape[0]
    if combine:
        slot = [0]

        def token_major(ref, g):
            d = DILATIONS[g]
            if d == 1:
                return ref[...].astype(F32)
            nch = ref.shape[-1] // LANES
            base = slot[0]
            slot[0] += nch
            for r in range(d):
                t = ref[r].astype(F32)
                for c in range(nch):
                    il_ref[base + c, pl.ds(r, tm // d, stride=d), :] = t[:, c * LANES:(c + 1) * LANES]
            return jnp.concatenate([il_ref[base + c] for c in range(nch)], axis=-1)

        ls = [token_major(l_refs[g], g) for g in range(N_GROUPS)]
        m = jnp.maximum(jnp.maximum(ls[0], ls[1]), ls[2])
        es = [jnp.exp(t - m) for t in ls]
        inv = 1.0 / (es[0] + es[1] + es[2])
        o = None
        for g in range(N_GROUPS):
            alpha = es[g] * inv
            hi = alpha.astype(BF)
            lo = (alpha - hi.astype(F32)).astype(BF)
            wide = jnp.dot(jnp.concatenate([hi, lo], axis=-1), e_ref[...], preferred_element_type=F32)
            t = wide * token_major(o_refs[g], g)
            o = t if o is None else o + t
    else:
        o = o_ref[...]
    a = jnp.dot(o.astype(BF), wa_ref[...], preferred_element_type=F32)
    ca = _ln_swish(cv_ref[...].astype(F32), lg_ref[...], lb_ref[...]).astype(BF)
    c = jnp.dot(ca, wc_ref[...], preferred_element_type=F32)
    merged = sg_ref[:, :D_MODEL].astype(F32) * a + sg_ref[:, D_MODEL:].astype(F32) * c
    h = x_ref[...] + jnp.dot(merged.astype(BF), wo_ref[...], preferred_element_type=F32)
    hn = (h * lax.rsqrt(jnp.mean(h * h, axis=-1, keepdims=True) + EPS) * g2_ref[...]).astype(BF)
    fch = 512
    for c0 in range(0, D_FF, fch):
        t = jnp.maximum(jnp.dot(hn, w1_ref[:, c0:c0 + fch], preferred_element_type=F32), 0.0)
        hid_ref[:, c0:c0 + fch] = (t * t).astype(BF)
    y_ref[...] = h + jnp.dot(hid_ref[...], w2_ref[...], preferred_element_type=F32)


def _main(x2d, attn, ca, sg, weights, tm, nt=1):
    M = x2d.shape[0]
    assert M % (tm * nt) == 0
    B = M // (tm * nt)
    row = lambda width: pl.BlockSpec((tm, width), lambda b, i: (b * nt + i, 0))
    combine = isinstance(attn, tuple)
    scratch = [pltpu.VMEM((tm, D_FF), BF)]
    if combine:
        os_, ls_ = attn

        def group_spec(d, width):
            if d == 1:
                return row(width)
            return pl.BlockSpec((None, d, tm // d, width), lambda b, i: (b, 0, i, 0))

        head = jnp.arange(ATTN_W) // HEAD_DIM
        spread = (jnp.arange(LANES)[:, None] == head[None, :]).astype(BF)
        attn_args = list(os_) + list(ls_) + [jnp.concatenate([spread, spread], axis=0)]
        attn_specs = ([group_spec(d, ATTN_W) for d in DILATIONS] + [group_spec(d, LANES) for d in DILATIONS]
                      + [_resident((2 * LANES, ATTN_W))])
        n_il = sum((ATTN_W + LANES) // LANES for d in DILATIONS if d > 1)
        scratch.append(pltpu.VMEM((n_il, tm, LANES), F32))
    else:
        attn_args, attn_specs = [attn], [row(ATTN_W)]
    return pl.pallas_call(
        functools.partial(_main_kernel, combine),
        grid=(B, nt),
        in_specs=[row(D_MODEL)] + attn_specs + [row(CONV_CH), row(2 * D_MODEL)]
                 + [_resident(w.shape) for w in weights],
        out_specs=row(D_MODEL),
        out_shape=jax.ShapeDtypeStruct((M, D_MODEL), F32),
        scratch_shapes=scratch,
        compiler_params=_params(2),
        name="main_prompt" if combine else "main_sample",
    )(x2d, *attn_args, ca, sg, *weights)


def kernel(x_prompt, x_sample, cache_k1, cache_v1, cache_k2, cache_v2, cache_k3, cache_v3, state_conv,
           norm1_g, w_in, q_norm_g, k_norm_g, conv_w, conv_b, conv_ln_g, conv_ln_b,
           w_attn_out, w_conv_out, w_o, norm2_g, w_ff1, w_ff2):
    assert w_in.shape[0] == 1, "one layer"
    B, S, _ = x_prompt.shape
    nb, ns, _ = x_sample.shape
    assert ns == 1, "one new token per sample row"
    tm = 512
    consts = _inproj_consts(norm1_g[0], w_in[0], q_norm_g[0], k_norm_g[0])
    conv_args = (conv_w[0], conv_b)
    weights = (conv_ln_g, conv_ln_b, w_attn_out[0].astype(BF), w_conv_out[0].astype(BF), w_o[0].astype(BF),
               norm2_g, w_ff1[0].astype(BF), w_ff2[0].astype(BF))

    outs = _inproj_prompt(x_prompt, consts, tm)
    qs, ks, vs = outs[0:3], outs[3:6], outs[6:9]
    u, sg = outs[9:11]
    kts, vts, ut = outs[11:14], outs[14:17], outs[17]
    attn_o, attn_l = [], []
    for g, d in enumerate(DILATIONS):
        as4 = lambda t: t.reshape(B, d, S // d, ATTN_W)
        o, lse = _attention_prompt(as4(qs[g]), as4(ks[g]), as4(vs[g]), g)
        attn_o.append(o.reshape(B * S, ATTN_W) if d == 1 else o)
        attn_l.append(lse.reshape(B * S, LANES) if d == 1 else lse)
    tail = lambda t: t.reshape(1, B, t.shape[1], N_HEADS, HEAD_DIM)
    p_conv = ut[None, :, HALO - (CONV_TAPS - 1):, :]

    q_s, k_s, v_s, u_s, sg_s = _inproj_sample(x_sample.reshape(nb, D_MODEL), consts)
    to_buf = lambda c: c[0].transpose(0, 2, 3, 1).reshape(nb, ATTN_W, c.shape[2])
    from_buf = lambda c: c.reshape(nb, N_HEADS, HEAD_DIM, c.shape[2]).transpose(0, 3, 1, 2)[None]
    o_s, bufs, ca = _attention_sample(q_s, k_s, v_s, [to_buf(c) for c in
                                                      (cache_k1, cache_v1, cache_k2, cache_v2, cache_k3, cache_v3)],
                                      conv=(u, S, *conv_args))
    if ca is None:
        ca = _conv_prompt(u, B, S, *conv_args, tc=tm)
    y_prompt = _main(x_prompt.reshape(B * S, D_MODEL), (tuple(attn_o), tuple(attn_l)), ca, sg, weights,
                     tm=tm, nt=S // tm).reshape(B, S, D_MODEL)
    ca_s, state_new = _conv_sample(state_conv[0].transpose(1, 0, 2), u_s, *conv_args, bb=min(32, nb))
    y_sample = _main(x_sample.reshape(nb, D_MODEL), o_s, ca_s, sg_s, weights, tm=nb).reshape(nb, 1, D_MODEL)
    s_conv = state_new.transpose(1, 0, 2)[None]

    kv_tails = [tail(t) for pair in zip(kts, vts) for t in pair]
    return (y_prompt, y_sample, *kv_tails, p_conv, *(from_buf(b) for b in bufs), s_conv)
```
